```python
import jax, jax.numpy as jnp
from jax import lax
import numpy as np

D_MODEL = 1024
BATCH = 16
SEQ = 4096
DEPTH = 1

CHUNK = 64
MIX_W = D_MODEL
POOL_W = MIX_W // 2
CONV_W = MIX_W - POOL_W
POOL_WINDOWS = (2, 4, 8, 16)
N_POOL_GROUPS = len(POOL_WINDOWS)
POOL_GROUP_W = POOL_W // N_POOL_GROUPS
N_CONV_GROUPS = 4
CONV_K = 3
IN_PROJ_W = POOL_W + 3 * CONV_W
MEM_LEN = 256
N_XHEADS = 4
XHEAD_DIM = D_MODEL // N_XHEADS
N_EXPERTS = 32
TOP_K = 4
D_FF = D_MODEL
SWIGLU_LIMIT = 7.0
SWIGLU_ALPHA = 1.702
EXPERT_BLOCK = 256
RMS_EPS = 1e-5

kernel_name = "hybrid_pool_conv_xattn_moe_encoder"


def rmsnorm(x, g):
    xf = x.astype(jnp.float32)
    y = xf * lax.rsqrt(jnp.mean(xf * xf, axis=-1, keepdims=True) + RMS_EPS)
    return (y * g.astype(jnp.float32)).astype(x.dtype)


def pool_mixer(u, w_pool, pool_scale):
    b, s, _ = u.shape
    ug = u.reshape(b, s, N_POOL_GROUPS, POOL_GROUP_W)
    cs = jnp.cumsum(ug.astype(jnp.float32), axis=1)
    pos = jnp.arange(1, s + 1, dtype=jnp.float32)
    outs = []
    for g, w in enumerate(POOL_WINDOWS):
        c = cs[:, :, g]
        prev = jnp.pad(c, ((0, 0), (w, 0), (0, 0)))[:, :s]
        cnt = jnp.minimum(pos, float(w))[None, :, None]
        outs.append((c - prev) / cnt - ug[:, :, g].astype(jnp.float32))
    d = jnp.stack(outs, axis=2).astype(u.dtype)
    y = jnp.einsum('bsgc,gcd->bsgd', d, w_pool).reshape(b, s, POOL_W)
    return y * pool_scale


def gated_short_conv(gb, gc, v, conv_w):
    z = gc * v
    rhs = conv_w[:, None, :]
    zc = lax.conv_general_dilated(z, rhs, window_strides=(1,), padding=[(CONV_K - 1, 0)],
                                  dimension_numbers=('NWC', 'WIO', 'NWC'),
                                  feature_group_count=CONV_W)
    return gb * zc


def mem_cross_attention(hn, mn, w_q, w_k, w_v, w_o):
    b, s, _ = hn.shape
    m = mn.shape[1]
    q = (hn @ w_q).reshape(b, s, N_XHEADS, XHEAD_DIM)
    k = (mn @ w_k).reshape(b, m, N_XHEADS, XHEAD_DIM)
    v = (mn @ w_v).reshape(b, m, N_XHEADS, XHEAD_DIM)
    sc = jnp.einsum('bshd,bmhd->bhsm', q, k).astype(jnp.float32) * (XHEAD_DIM ** -0.5)
    p = jax.nn.softmax(sc, axis=-1).astype(v.dtype)
    o = jnp.einsum('bhsm,bmhd->bshd', p, v).reshape(b, s, D_MODEL)
    return o @ w_o


def routed_experts(hn, w_router, b_router, w_gate_up, b_gate_up, w_down, b_down):
    b, s, d = hn.shape
    t = b * s
    xt = hn.reshape(t, d)
    logits = xt.astype(jnp.float32) @ w_router.astype(jnp.float32) + b_router.astype(jnp.float32)
    top_val, top_idx = lax.top_k(logits, TOP_K)
    gates = jax.nn.softmax(top_val, axis=-1)
    a = t * TOP_K
    flat_e = top_idx.reshape(a).astype(jnp.int32)
    flat_tok = jnp.repeat(jnp.arange(t, dtype=jnp.int32), TOP_K)
    flat_gate = gates.reshape(a)
    order = jnp.argsort(flat_e)
    sorted_e = flat_e[order]
    counts = jnp.bincount(flat_e, length=N_EXPERTS).astype(jnp.int32)
    starts = jnp.cumsum(counts) - counts
    padded = ((counts + EXPERT_BLOCK - 1) // EXPERT_BLOCK) * EXPERT_BLOCK
    pstarts = jnp.cumsum(padded) - padded
    rank = jnp.arange(a, dtype=jnp.int32) - starts[sorted_e]
    dest = pstarts[sorted_e] + rank
    p_len = a + N_EXPERTS * EXPERT_BLOCK
    n_blocks = p_len // EXPERT_BLOCK
    tok_buf = jnp.zeros((p_len,), jnp.int32).at[dest].set(flat_tok[order])
    gate_buf = jnp.zeros((p_len,), jnp.float32).at[dest].set(flat_gate[order])
    block_start = jnp.arange(n_blocks, dtype=jnp.int32) * EXPERT_BLOCK
    block_e = jnp.clip(jnp.searchsorted(pstarts, block_start, side='right') - 1, 0, N_EXPERTS - 1).astype(jnp.int32)

    def block_step(acc, blk):
        tok, g, e = blk
        xb = xt[tok]
        gu = xb @ w_gate_up[e] + b_gate_up[e]
        x_glu = jnp.minimum(gu[:, :D_FF], SWIGLU_LIMIT)
        x_lin = jnp.clip(gu[:, D_FF:], -SWIGLU_LIMIT, SWIGLU_LIMIT)
        act = (x_lin + 1) * (x_glu * jax.nn.sigmoid(SWIGLU_ALPHA * x_glu))
        y = act @ w_down[e] + b_down[e]
        acc = acc.at[tok].add((y.astype(jnp.float32) * g[:, None]).astype(acc.dtype))
        return acc, None

    acc, _ = lax.scan(block_step, jnp.zeros_like(xt),
                      (tok_buf.reshape(n_blocks, EXPERT_BLOCK),
                       gate_buf.reshape(n_blocks, EXPERT_BLOCK), block_e))
    return acc.reshape(b, s, d)


def setup_inputs(seed: int = 0) -> dict:
    key = jax.random.key(seed)
    ks = jax.random.split(key, 24)
    f32 = jnp.float32
    nrm = lambda k, shape, scale: jax.random.normal(k, shape, f32) * scale
    gain = lambda k, shape: 1.0 + 0.05 * jax.random.normal(k, shape, f32)
    L = DEPTH
    return {
        "x": jax.random.normal(ks[0], (BATCH, SEQ, D_MODEL), f32),
        "mem": jax.random.normal(ks[1], (BATCH, MEM_LEN, D_MODEL), f32),
        "g_mix": gain(ks[2], (L, D_MODEL)),
        "w_in": nrm(ks[3], (L, D_MODEL, IN_PROJ_W), D_MODEL ** -0.5),
        "w_pool": nrm(ks[4], (L, N_POOL_GROUPS, POOL_GROUP_W, POOL_GROUP_W), POOL_GROUP_W ** -0.5),
        "pool_scale": 1.0 + 0.1 * jax.random.normal(ks[5], (L, POOL_W), f32),
        "conv_w": nrm(ks[6], (L, CONV_K, CONV_W), CONV_K ** -0.5),
        "w_out": nrm(ks[7], (L, MIX_W, D_MODEL), MIX_W ** -0.5),
        "g_xattn": gain(ks[8], (L, D_MODEL)),
        "g_mem": gain(ks[9], (L, D_MODEL)),
        "w_q": nrm(ks[10], (L, D_MODEL, D_MODEL), D_MODEL ** -0.5),
        "w_k": nrm(ks[11], (L, D_MODEL, D_MODEL), D_MODEL ** -0.5),
        "w_v": nrm(ks[12], (L, D_MODEL, D_MODEL), D_MODEL ** -0.5),
        "w_o": nrm(ks[13], (L, D_MODEL, D_MODEL), D_MODEL ** -0.5),
        "g_moe": gain(ks[14], (L, D_MODEL)),
        "w_router": nrm(ks[15], (L, D_MODEL, N_EXPERTS), D_MODEL ** -0.5),
        "b_router": nrm(ks[16], (L, N_EXPERTS), 0.01),
        "w_gate_up": nrm(ks[17], (L, N_EXPERTS, D_MODEL, 2 * D_FF), D_MODEL ** -0.5),
        "b_gate_up": nrm(ks[18], (L, N_EXPERTS, 2 * D_FF), 0.01),
        "w_down": nrm(ks[19], (L, N_EXPERTS, D_FF, D_MODEL), D_FF ** -0.5),
        "b_down": nrm(ks[20], (L, N_EXPERTS, D_MODEL), 0.01),
        "g_final": gain(ks[21], (D_MODEL,)),
    }


def reference(x, mem, g_mix, w_in, w_pool, pool_scale, conv_w, w_out,
              g_xattn, g_mem, w_q, w_k, w_v, w_o,
              g_moe, w_router, b_router, w_gate_up, b_gate_up, w_down, b_down,
              g_final):
    h = x
    for l in range(DEPTH):
        hn = rmsnorm(h, g_mix[l])
        proj = hn @ w_in[l]
        u = proj[..., :POOL_W]
        gb = proj[..., POOL_W:POOL_W + CONV_W]
        gc = proj[..., POOL_W + CONV_W:POOL_W + 2 * CONV_W]
        v = proj[..., POOL_W + 2 * CONV_W:]
        y_pool = pool_mixer(u, w_pool[l], pool_scale[l])
        y_conv = gated_short_conv(gb, gc, v, conv_w[l])
        h = h + jnp.concatenate([y_pool, y_conv], axis=-1) @ w_out[l]
        h = h + mem_cross_attention(rmsnorm(h, g_xattn[l]), rmsnorm(mem, g_mem[l]),
                                    w_q[l], w_k[l], w_v[l], w_o[l])
        h = h + routed_experts(rmsnorm(h, g_moe[l]), w_router[l], b_router[l],
                               w_gate_up[l], b_gate_up[l], w_down[l], b_down[l])
    return rmsnorm(h, g_final)
```

```python
import functools

import jax
import jax.numpy as jnp
from jax import lax
from jax.experimental import pallas as pl
from jax.experimental.pallas import tpu as pltpu

POOL_WINDOWS = (2, 4, 8, 16)
POOL_GROUP_W = 128
POOL_W = 512
CONV_W = 512
CONV_K = 3
N_XHEADS = 4
N_EXPERTS = 32
TOP_K = 4
SWIGLU_LIMIT = 7.0
SWIGLU_ALPHA = 1.702
RMS_EPS = 1e-5

LANES = 128
SEQ_TILE = 512
EXPERT_ROWS = 512
FF_CHUNK = 512
HALO = 16
VMEM_LIMIT = 56 * 1024 * 1024

F32 = jnp.float32
BF16 = jnp.bfloat16


def _rms(x, g):
    ms = jnp.mean(x * x, axis=-1, keepdims=True)
    return x * lax.rsqrt(ms + RMS_EPS) * g


def _dot(a, b):
    return jnp.dot(a, b, preferred_element_type=F32)


def _kv_kernel(mem_ref, g_ref, wk_ref, wv_ref, k_ref, v_ref):
    mn = _rms(mem_ref[...], g_ref[...]).astype(BF16)
    k_ref[...] = _dot(mn, wk_ref[...]).astype(BF16)
    v_ref[...] = _dot(mn, wv_ref[...]).astype(BF16)


def _kv_call(mem, g_mem, w_k, w_v):
    b, m, d = mem.shape
    full = lambda shape: pl.BlockSpec(shape, lambda i: (0,) * len(shape))
    return pl.pallas_call(
        _kv_kernel,
        grid=(b,),
        in_specs=[pl.BlockSpec((None, m, d), lambda i: (i, 0, 0)), full((1, d)), full((d, d)), full((d, d))],
        out_specs=[pl.BlockSpec((None, m, d), lambda i: (i, 0, 0))] * 2,
        out_shape=[jax.ShapeDtypeStruct((b, m, d), BF16)] * 2,
        compiler_params=pltpu.CompilerParams(dimension_semantics=("arbitrary",), vmem_limit_bytes=VMEM_LIMIT),
        name="kv_proj",
    )(mem, g_mem, w_k, w_v)


def _token_kernel(x_ref, gmix_ref, win_ref, wpool_ref, pscale_ref, convw_ref, wout_ref,
                  gx_ref, k_ref, v_ref, wq_ref, wo_ref, gmoe_ref, wr_ref, br_ref, tri_ref,
                  h2_ref, eidx_ref, gate_ref, rank_ref, cnt_ref,
                  uhalo, zhalo, carry):
    b = pl.program_id(0)
    s = pl.program_id(1)
    ts = x_ref.shape[0]

    @pl.when(s == 0)
    def _():
        uhalo[...] = jnp.zeros_like(uhalo)
        zhalo[...] = jnp.zeros_like(zhalo)

    @pl.when(jnp.logical_and(b == 0, s == 0))
    def _():
        carry[...] = jnp.zeros_like(carry)

    x = x_ref[...]
    hn = _rms(x, gmix_ref[...]).astype(BF16)
    proj = _dot(hn, win_ref[...])
    u = proj[:, :POOL_W]
    gb = proj[:, POOL_W:POOL_W + CONV_W]
    gc = proj[:, POOL_W + CONV_W:POOL_W + 2 * CONV_W]
    v = proj[:, POOL_W + 2 * CONV_W:]

    pos = (s * ts + 1 + lax.broadcasted_iota(jnp.int32, (ts, 1), 0)).astype(F32)
    ucat = jnp.concatenate([uhalo[...], u], axis=0)
    mix = None
    for g, w in enumerate(POOL_WINDOWS):
        cols = slice(g * POOL_GROUP_W, (g + 1) * POOL_GROUP_W)
        win = ucat[:, cols]
        span = 1
        while span < w:
            win = win + pltpu.roll(win, span, 0)
            span *= 2
        inv_cnt = 1.0 / jnp.minimum(pos, float(w))
        dg = win[HALO:, :] * inv_cnt - u[:, cols]
        yg = _dot(dg.astype(BF16), wpool_ref[g]) * pscale_ref[:, cols]
        part = _dot(yg.astype(BF16), wout_ref[cols, :])
        mix = part if mix is None else mix + part
    uhalo[...] = u[ts - HALO:, :]

    z = gc * v
    zcat = jnp.concatenate([zhalo[...], z], axis=0)
    z1 = pltpu.roll(zcat, 1, 0)[8:, :]
    z2 = pltpu.roll(zcat, 2, 0)[8:, :]
    zc = convw_ref[0:1, :] * z2 + convw_ref[1:2, :] * z1 + convw_ref[2:3, :] * z
    zhalo[...] = z[ts - 8:, :]
    y_conv = gb * zc
    mix = mix + _dot(y_conv.astype(BF16), wout_ref[POOL_W:, :])
    h1 = x + mix

    hq = _rms(h1, gx_ref[...]).astype(BF16)
    q = _dot(hq, wq_ref[...]).astype(BF16)
    hd = q.shape[1] // N_XHEADS
    attn = None
    for h in range(N_XHEADS):
        cols = slice(h * hd, (h + 1) * hd)
        sc = lax.dot_general(q[:, cols], k_ref[:, cols], (((1,), (1,)), ((), ())),
                             preferred_element_type=F32) * (hd ** -0.5)
        sc = sc - jnp.max(sc, axis=-1, keepdims=True)
        p = jnp.exp(sc)
        p = p / jnp.sum(p, axis=-1, keepdims=True)
        oh = _dot(p.astype(BF16), v_ref[:, cols])
        part = _dot(oh.astype(BF16), wo_ref[cols, :])
        attn = part if attn is None else attn + part
    h2 = h1 + attn
    h2_ref[...] = h2

    hm = _rms(h2, gmoe_ref[...]).astype(BF16)
    logits = _dot(hm, wr_ref[...]) + br_ref[...]
    ne = logits.shape[1]
    lane = lax.broadcasted_iota(jnp.int32, (ts, ne), 1).astype(F32)
    sel = jnp.zeros((ts, ne), F32)
    vals, idxs, hots = [], [], []
    for _ in range(TOP_K):
        m = jnp.max(logits, axis=-1, keepdims=True)
        idx = jnp.min(jnp.where(logits == m, lane, float(ne)), axis=-1, keepdims=True)
        hot = lane == idx
        logits = jnp.where(hot, -jnp.inf, logits)
        sel = jnp.where(hot, 1.0, sel)
        vals.append(m)
        idxs.append(idx)
        hots.append(hot)
    exps = [jnp.exp(vk - vals[0]) for vk in vals]
    denom = exps[0]
    for ek in exps[1:]:
        denom = denom + ek
    gates = [ek / denom for ek in exps]

    before = _dot(tri_ref[...], sel.astype(BF16)) + carry[...]
    ranks = [jnp.sum(jnp.where(hot, before, 0.0), axis=-1, keepdims=True) for hot in hots]
    carry[...] = carry[...] + jnp.sum(sel, axis=0, keepdims=True)
    cnt_ref[...] = carry[...]

    out_lane = lax.broadcasted_iota(jnp.int32, (ts, LANES), 1)

    def spread(cols_):
        acc = jnp.zeros((ts, LANES), F32)
        for kk, c in enumerate(cols_):
            acc = jnp.where(out_lane == kk, c, acc)
        return acc

    eidx_ref[...] = spread(idxs).astype(jnp.int32)
    gate_ref[...] = spread(gates)
    rank_ref[...] = spread(ranks).astype(jnp.int32)


def _token_call(x, kmem, vmem, g_mix, w_in, w_pool, pool_scale, conv_w, w_out, g_xattn, w_q, w_o,
                g_moe, w_router, b_router):
    b, s, d = x.shape
    ts = SEQ_TILE
    ns = s // ts
    t = b * s
    m = kmem.shape[1]
    ne = w_router.shape[1]
    tri = (lax.broadcasted_iota(jnp.int32, (ts, ts), 1) < lax.broadcasted_iota(jnp.int32, (ts, ts), 0)).astype(BF16)

    def full(shape):
        return pl.BlockSpec(shape, lambda i, j: (0,) * len(shape))

    tok_out = pl.BlockSpec((ts, LANES), lambda i, j: (i * ns + j, 0))
    return pl.pallas_call(
        _token_kernel,
        grid=(b, ns),
        in_specs=[
            pl.BlockSpec((None, ts, d), lambda i, j: (i, j, 0)),
            full((1, d)), full(w_in.shape), full(w_pool.shape), full((1, POOL_W)), full(conv_w.shape),
            full(w_out.shape), full((1, d)),
            pl.BlockSpec((None, m, d), lambda i, j: (i, 0, 0)),
            pl.BlockSpec((None, m, d), lambda i, j: (i, 0, 0)),
            full(w_q.shape), full(w_o.shape), full((1, d)), full(w_router.shape), full((1, ne)), full((ts, ts)),
        ],
        out_specs=[
            pl.BlockSpec((None, ts, d), lambda i, j: (i, j, 0)),
            tok_out, tok_out, tok_out,
            pl.BlockSpec((1, ne), lambda i, j: (0, 0)),
        ],
        out_shape=[
            jax.ShapeDtypeStruct((b, s, d), F32),
            jax.ShapeDtypeStruct((t, LANES), jnp.int32),
            jax.ShapeDtypeStruct((t, LANES), F32),
            jax.ShapeDtypeStruct((t, LANES), jnp.int32),
            jax.ShapeDtypeStruct((1, ne), F32),
        ],
        scratch_shapes=[pltpu.VMEM((HALO, POOL_W), F32), pltpu.VMEM((8, CONV_W), F32), pltpu.VMEM((1, ne), F32)],
        compiler_params=pltpu.CompilerParams(dimension_semantics=("arbitrary", "arbitrary"),
                                             vmem_limit_bytes=VMEM_LIMIT),
        name="token_mix_attn_route",
    )(x, g_mix, w_in, w_pool, pool_scale, conv_w, w_out, g_xattn, kmem, vmem, w_q, w_o, g_moe, w_router,
      b_router, tri)


def _expert_kernel(be_ref, nused_ref,
                   tok_hbm, dst_hbm, h2_hbm, gmoe_ref, wgu_ref, bgu_ref, wd_ref, bd_ref,
                   y4_hbm,
                   tok_s, dst_s, xbuf, ybuf, act, sems):
    j = pl.program_id(0)
    rows = xbuf.shape[0]
    d_ff = wd_ref.shape[0]

    @pl.when(j == 0)
    def _():
        ybuf[...] = jnp.zeros_like(ybuf)
        spare0 = y4_hbm.shape[0] - 2 * rows
        for half in range(2):
            fill = pltpu.make_async_copy(ybuf, y4_hbm.at[pl.ds(spare0 + half * rows, rows)], sems.at[3])
            fill.start()
            fill.wait()

    @pl.when(j < nused_ref[0])
    def _():
        ctok = pltpu.make_async_copy(tok_hbm.at[j], tok_s, sems.at[0])
        cdst = pltpu.make_async_copy(dst_hbm.at[j], dst_s, sems.at[1])
        ctok.start()
        cdst.start()
        ctok.wait()
        cdst.wait()

        def gather(i, c):
            tkn = tok_s[0, i]
            pltpu.make_async_copy(h2_hbm.at[pl.ds(tkn, 1)], xbuf.at[pl.ds(i, 1)], sems.at[2]).start()
            return c

        lax.fori_loop(0, rows, gather, 0)
        pltpu.make_async_copy(h2_hbm.at[pl.ds(0, rows)], xbuf, sems.at[2]).wait()

        hn = _rms(xbuf[...], gmoe_ref[...]).astype(BF16)
        for c in range(d_ff // FF_CHUNK):
            lo = c * FF_CHUNK
            glu = _dot(hn, wgu_ref[:, lo:lo + FF_CHUNK]) + bgu_ref[:, lo:lo + FF_CHUNK]
            lin = _dot(hn, wgu_ref[:, d_ff + lo:d_ff + lo + FF_CHUNK]) + bgu_ref[:, d_ff + lo:d_ff + lo + FF_CHUNK]
            glu = jnp.minimum(glu, SWIGLU_LIMIT)
            lin = jnp.clip(lin, -SWIGLU_LIMIT, SWIGLU_LIMIT)
            a = (lin + 1.0) * (glu * jax.nn.sigmoid(SWIGLU_ALPHA * glu))
            act[:, lo:lo + FF_CHUNK] = a.astype(BF16)
        ybuf[...] = _dot(act[...], wd_ref[...]) + bd_ref[...]

        def scatter(i, c):
            row = dst_s[0, i]
            pltpu.make_async_copy(ybuf.at[pl.ds(i, 1)], y4_hbm.at[pl.ds(row, 1)], sems.at[3]).start()
            return c

        lax.fori_loop(0, rows, scatter, 0)
        pltpu.make_async_copy(ybuf, y4_hbm.at[pl.ds(0, rows)], sems.at[3]).wait()


def _expert_call(block_e, n_used, tok_slot, dst_slot, h2, g_moe, w_gu, b_gu, w_d, b_d, out_rows):
    t, d = h2.shape
    nb = tok_slot.shape[0]
    rows = EXPERT_ROWS
    d_ff = w_d.shape[1]
    any_spec = pl.BlockSpec(memory_space=pl.ANY)
    grid_spec = pltpu.PrefetchScalarGridSpec(
        num_scalar_prefetch=2,
        grid=(nb,),
        in_specs=[
            any_spec, any_spec, any_spec,
            pl.BlockSpec((1, d), lambda j, be, nu: (0, 0)),
            pl.BlockSpec((None, d, 2 * d_ff), lambda j, be, nu: (be[j], 0, 0)),
            pl.BlockSpec((None, 1, 2 * d_ff), lambda j, be, nu: (be[j], 0, 0)),
            pl.BlockSpec((None, d_ff, d), lambda j, be, nu: (be[j], 0, 0)),
            pl.BlockSpec((None, 1, d), lambda j, be, nu: (be[j], 0, 0)),
        ],
        out_specs=any_spec,
        scratch_shapes=[
            pltpu.SMEM((1, rows), jnp.int32), pltpu.SMEM((1, rows), jnp.int32),
            pltpu.VMEM((rows, d), F32), pltpu.VMEM((rows, d), F32), pltpu.VMEM((rows, d_ff), BF16),
            pltpu.SemaphoreType.DMA((4,)),
        ],
    )
    return pl.pallas_call(
        _expert_kernel,
        grid_spec=grid_spec,
        out_shape=jax.ShapeDtypeStruct((out_rows, d), F32),
        compiler_params=pltpu.CompilerParams(dimension_semantics=("arbitrary",), vmem_limit_bytes=VMEM_LIMIT),
        name="expert_ffn",
    )(block_e, n_used, tok_slot, dst_slot, h2, g_moe, w_gu, b_gu, w_d, b_d)


def _combine_kernel(h2_ref, y0_ref, y1_ref, y2_ref, y3_ref, gate_ref, g_ref, o_ref):
    gts = gate_ref[...]
    acc = y0_ref[...] * gts[:, 0:1]
    acc = acc + y1_ref[...] * gts[:, 1:2]
    acc = acc + y2_ref[...] * gts[:, 2:3]
    acc = acc + y3_ref[...] * gts[:, 3:4]
    o_ref[...] = _rms(h2_ref[...] + acc, g_ref[...])


def _combine_call(h2, y4, gate, g_final):
    t, d = h2.shape
    ts = SEQ_TILE
    nt = t // ts
    plane = lambda k: pl.BlockSpec((ts, d), lambda i: (k * nt + i, 0))
    return pl.pallas_call(
        _combine_kernel,
        grid=(nt,),
        in_specs=[pl.BlockSpec((ts, d), lambda i: (i, 0)), plane(0), plane(1), plane(2), plane(3),
                  pl.BlockSpec((ts, LANES), lambda i: (i, 0)), pl.BlockSpec((1, d), lambda i: (0, 0))],
        out_specs=pl.BlockSpec((ts, d), lambda i: (i, 0)),
        out_shape=jax.ShapeDtypeStruct((t, d), F32),
        compiler_params=pltpu.CompilerParams(dimension_semantics=("arbitrary",), vmem_limit_bytes=VMEM_LIMIT),
        name="combine_norm",
    )(h2, y4, y4, y4, y4, gate, g_final)


def _slot_tables(eidx, rank, counts, t):
    rows = EXPERT_ROWS
    n_slots = TOP_K * t + N_EXPERTS * rows
    nb = n_slots // rows
    padded = ((counts + rows - 1) // rows) * rows
    pstarts = jnp.cumsum(padded) - padded
    dest = pstarts[eidx] + rank
    flat = jnp.arange(t * TOP_K, dtype=jnp.int32)
    slot_a = jnp.full((n_slots,), -1, jnp.int32).at[dest.reshape(-1)].set(flat, unique_indices=True)
    valid = slot_a >= 0
    tok_slot = jnp.where(valid, slot_a // TOP_K, 0)
    spare = TOP_K * t + jnp.arange(n_slots, dtype=jnp.int32) % (2 * rows)
    dst_slot = jnp.where(valid, (slot_a % TOP_K) * t + slot_a // TOP_K, spare)
    n_used = (jnp.sum(padded) // rows).astype(jnp.int32)
    starts = jnp.arange(nb, dtype=jnp.int32) * rows
    block_e = jnp.clip(jnp.searchsorted(pstarts, starts, side="right") - 1, 0, N_EXPERTS - 1).astype(jnp.int32)
    last_e = block_e[jnp.maximum(n_used - 1, 0)]
    block_e = jnp.where(jnp.arange(nb) < n_used, block_e, last_e)
    return (block_e, n_used.reshape(1), tok_slot.reshape(nb, 1, rows), dst_slot.reshape(nb, 1, rows),
            TOP_K * t + 2 * rows)


def kernel(x, mem, g_mix, w_in, w_pool, pool_scale, conv_w, w_out, g_xattn, g_mem, w_q, w_k, w_v, w_o,
           g_moe, w_router, b_router, w_gate_up, b_gate_up, w_down, b_down, g_final):
    b, s, d = x.shape
    t = b * s
    assert g_mix.shape[0] == 1, "the combine kernel fuses the final norm, so exactly one layer is supported"
    assert s % SEQ_TILE == 0 and d % LANES == 0
    row = lambda a: a[0].reshape(1, -1)
    kmem, vmem = _kv_call(mem, row(g_mem), w_k[0].astype(BF16), w_v[0].astype(BF16))
    h2, eidx, gate, rank, counts = _token_call(
        x, kmem, vmem, row(g_mix), w_in[0].astype(BF16), w_pool[0].astype(BF16), row(pool_scale), conv_w[0],
        w_out[0].astype(BF16), row(g_xattn), w_q[0].astype(BF16), w_o[0].astype(BF16), row(g_moe),
        w_router[0].astype(BF16), row(b_router))
    h2 = h2.reshape(t, d)
    block_e, n_used, tok_slot, dst_slot, out_rows = _slot_tables(
        eidx[:, :TOP_K], rank[:, :TOP_K], counts[0].astype(jnp.int32), t)
    y4 = _expert_call(block_e, n_used, tok_slot, dst_slot, h2, row(g_moe),
                      w_gate_up[0].astype(BF16), b_gate_up[0][:, None, :],
                      w_down[0].astype(BF16), b_down[0][:, None, :], out_rows)
    return _combine_call(h2, y4, gate, g_final.reshape(1, -1)).reshape(b, s, d)
```

```python
import jax
import jax.numpy as jnp
from jax import lax
from jax.experimental import pallas as pl
from jax.experimental.pallas import tpu as pltpu

POOL_WINDOWS = (2, 4, 8, 16)
POOL_GROUP_W = 128
POOL_W = 512
CONV_W = 512
N_XHEADS = 4
N_EXPERTS = 32
TOP_K = 4
SWIGLU_LIMIT = 7.0
SWIGLU_ALPHA = 1.702
RMS_EPS = 1e-5

LANES = 128
ROW_TILE = 8
SEQ_TILE = 512
EXPERT_ROWS = 512
FF_CHUNK = 256
HALO = 16
VMEM_LIMIT = 56 * 1024 * 1024

KEY_ID_BITS = 18
KEY_PAD_FLAG = 1 << KEY_ID_BITS
KEY_EXPERT_SHIFT = KEY_ID_BITS + 1

F32 = jnp.float32
BF16 = jnp.bfloat16


def _rms(x, g):
    ms = jnp.mean(x * x, axis=-1, keepdims=True)
    return x * lax.rsqrt(ms + RMS_EPS) * g


def _dot(a, b):
    return jnp.dot(a, b, preferred_element_type=F32)


def _kv_kernel(mem_ref, g_ref, wk_ref, wv_ref, k_ref, v_ref):
    mn = _rms(mem_ref[...], g_ref[...]).astype(BF16)
    k_ref[...] = _dot(mn, wk_ref[...]).astype(BF16)
    v_ref[...] = _dot(mn, wv_ref[...]).astype(BF16)


def _kv_call(mem, g_mem, w_k, w_v):
    b, m, d = mem.shape
    full = lambda shape: pl.BlockSpec(shape, lambda i: (0,) * len(shape))
    return pl.pallas_call(
        _kv_kernel,
        grid=(b,),
        in_specs=[pl.BlockSpec((None, m, d), lambda i: (i, 0, 0)), full((1, d)), full((d, d)), full((d, d))],
        out_specs=[pl.BlockSpec((None, m, d), lambda i: (i, 0, 0))] * 2,
        out_shape=[jax.ShapeDtypeStruct((b, m, d), BF16)] * 2,
        compiler_params=pltpu.CompilerParams(dimension_semantics=("arbitrary",), vmem_limit_bytes=VMEM_LIMIT),
        name="kv_proj",
    )(mem, g_mem, w_k, w_v)


def _token_kernel(x_ref, gmix_ref, win_ref, wpool_ref, pscale_ref, convw_ref, wout_ref,
                  gx_ref, k_ref, v_ref, wq_ref, wo_ref, gmoe_ref, wr_ref, br_ref,
                  h2_ref, eidx_ref, gate_ref, cnt_ref,
                  uhalo, zhalo, carry):
    b = pl.program_id(0)
    s = pl.program_id(1)
    ts = x_ref.shape[0]

    @pl.when(s == 0)
    def _():
        uhalo[...] = jnp.zeros_like(uhalo)
        zhalo[...] = jnp.zeros_like(zhalo)

    @pl.when(jnp.logical_and(b == 0, s == 0))
    def _():
        carry[...] = jnp.zeros_like(carry)

    x = x_ref[...]
    hn = _rms(x, gmix_ref[...]).astype(BF16)
    proj = _dot(hn, win_ref[...])
    u = proj[:, :POOL_W]
    gb = proj[:, POOL_W:POOL_W + CONV_W]
    gc = proj[:, POOL_W + CONV_W:POOL_W + 2 * CONV_W]
    v = proj[:, POOL_W + 2 * CONV_W:]

    pos = (s * ts + 1 + lax.broadcasted_iota(jnp.int32, (ts, 1), 0)).astype(F32)
    ucat = jnp.concatenate([uhalo[...], u], axis=0)
    mix = None
    for g, w in enumerate(POOL_WINDOWS):
        cols = slice(g * POOL_GROUP_W, (g + 1) * POOL_GROUP_W)
        win = ucat[:, cols]
        span = 1
        while span < w:
            win = win + pltpu.roll(win, span, 0)
            span *= 2
        inv_cnt = 1.0 / jnp.minimum(pos, float(w))
        dg = win[HALO:, :] * inv_cnt - u[:, cols]
        yg = _dot(dg.astype(BF16), wpool_ref[g]) * pscale_ref[:, cols]
        part = _dot(yg.astype(BF16), wout_ref[cols, :])
        mix = part if mix is None else mix + part
    uhalo[...] = u[ts - HALO:, :]

    z = gc * v
    zcat = jnp.concatenate([zhalo[...], z], axis=0)
    z1 = pltpu.roll(zcat, 1, 0)[8:, :]
    z2 = pltpu.roll(zcat, 2, 0)[8:, :]
    zc = convw_ref[0:1, :] * z2 + convw_ref[1:2, :] * z1 + convw_ref[2:3, :] * z
    zhalo[...] = z[ts - 8:, :]
    y_conv = gb * zc
    mix = mix + _dot(y_conv.astype(BF16), wout_ref[POOL_W:, :])
    h1 = x + mix

    hq = _rms(h1, gx_ref[...]).astype(BF16)
    q = _dot(hq, wq_ref[...]).astype(BF16)
    hd = q.shape[1] // N_XHEADS
    attn = None
    for h in range(N_XHEADS):
        cols = slice(h * hd, (h + 1) * hd)
        sc = lax.dot_general(q[:, cols], k_ref[:, cols], (((1,), (1,)), ((), ())),
                             preferred_element_type=F32) * (hd ** -0.5)
        sc = sc - jnp.max(sc, axis=-1, keepdims=True)
        p = jnp.exp(sc)
        p = p / jnp.sum(p, axis=-1, keepdims=True)
        oh = _dot(p.astype(BF16), v_ref[:, cols])
        part = _dot(oh.astype(BF16), wo_ref[cols, :])
        attn = part if attn is None else attn + part
    h2 = h1 + attn
    for c in range(ROW_TILE):
        h2_ref[pl.ds(c, ts, stride=ROW_TILE), :] = h2[:, c * LANES:(c + 1) * LANES]

    hm = _rms(h2, gmoe_ref[...]).astype(BF16)
    logits = _dot(hm, wr_ref[...]) + br_ref[...]
    ne = logits.shape[1]
    lane = lax.broadcasted_iota(jnp.int32, (ts, ne), 1).astype(F32)
    sel = jnp.zeros((ts, ne), F32)
    vals, idxs = [], []
    for _ in range(TOP_K):
        m = jnp.max(logits, axis=-1, keepdims=True)
        idx = jnp.min(jnp.where(logits == m, lane, float(ne)), axis=-1, keepdims=True)
        hot = lane == idx
        logits = jnp.where(hot, -jnp.inf, logits)
        sel = jnp.where(hot, 1.0, sel)
        vals.append(m)
        idxs.append(idx)
    exps = [jnp.exp(vk - vals[0]) for vk in vals]
    denom = exps[0]
    for ek in exps[1:]:
        denom = denom + ek
    gates = [ek / denom for ek in exps]

    carry[...] = carry[...] + jnp.sum(sel, axis=0, keepdims=True)
    cnt_ref[...] = carry[...]

    out_lane = lax.broadcasted_iota(jnp.int32, (ts, LANES), 1)

    def spread(cols_):
        acc = jnp.zeros((ts, LANES), F32)
        for kk, c in enumerate(cols_):
            acc = jnp.where(out_lane == kk, c, acc)
        return acc

    eidx_ref[...] = spread(idxs).astype(jnp.int32)
    gate_ref[...] = spread(gates)


def _token_call(x, kmem, vmem, g_mix, w_in, w_pool, pool_scale, conv_w, w_out, g_xattn, w_q, w_o,
                g_moe, w_router, b_router):
    b, s, d = x.shape
    ts = SEQ_TILE
    ns = s // ts
    t = b * s
    m = kmem.shape[1]
    ne = w_router.shape[1]

    def full(shape):
        return pl.BlockSpec(shape, lambda i, j: (0,) * len(shape))

    tok_out = pl.BlockSpec((ts, LANES), lambda i, j: (i * ns + j, 0))
    return pl.pallas_call(
        _token_kernel,
        grid=(b, ns),
        in_specs=[
            pl.BlockSpec((None, ts, d), lambda i, j: (i, j, 0)),
            full((1, d)), full(w_in.shape), full(w_pool.shape), full((1, POOL_W)), full(conv_w.shape),
            full(w_out.shape), full((1, d)),
            pl.BlockSpec((None, m, d), lambda i, j: (i, 0, 0)),
            pl.BlockSpec((None, m, d), lambda i, j: (i, 0, 0)),
            full(w_q.shape), full(w_o.shape), full((1, d)), full(w_router.shape), full((1, ne)),
        ],
        out_specs=[
            pl.BlockSpec((ts * ROW_TILE, LANES), lambda i, j: (i * ns + j, 0)),
            tok_out, tok_out,
            pl.BlockSpec((1, ne), lambda i, j: (0, 0)),
        ],
        out_shape=[
            jax.ShapeDtypeStruct((t * ROW_TILE, LANES), F32),
            jax.ShapeDtypeStruct((t, LANES), jnp.int32),
            jax.ShapeDtypeStruct((t, LANES), F32),
            jax.ShapeDtypeStruct((1, ne), F32),
        ],
        scratch_shapes=[pltpu.VMEM((HALO, POOL_W), F32), pltpu.VMEM((8, CONV_W), F32), pltpu.VMEM((1, ne), F32)],
        compiler_params=pltpu.CompilerParams(dimension_semantics=("arbitrary", "arbitrary"),
                                             vmem_limit_bytes=VMEM_LIMIT),
        name="token_mix_attn_route",
    )(x, g_mix, w_in, w_pool, pool_scale, conv_w, w_out, g_xattn, kmem, vmem, w_q, w_o, g_moe, w_router, b_router)


def _expert_kernel(be_ref, nused_ref,
                   tok_hbm, dst_hbm, h2_hbm, gmoe_ref, wgu_ref, bgu_ref, wd_ref, bd_ref,
                   y4_hbm,
                   tok_s, dst_s, xbuf, ybuf, act, gsem, ssem, isem):
    j = pl.program_id(0)
    nb = pl.num_programs(0)
    rows = act.shape[0]
    d_ff = wd_ref.shape[0]
    n_used = nused_ref[0]
    spare0 = y4_hbm.shape[0] - 3 * rows * ROW_TILE

    def table_copies(tok_blk, dst_blk, slot):
        return (pltpu.make_async_copy(tok_hbm.at[tok_blk], tok_s.at[pl.ds(slot, 1)], isem.at[slot]),
                pltpu.make_async_copy(dst_hbm.at[dst_blk], dst_s.at[pl.ds(slot, 1)], isem.at[slot]))

    def gather_row(slot, i):
        src = pl.multiple_of(tok_s[slot, i], ROW_TILE)
        return pltpu.make_async_copy(h2_hbm.at[pl.ds(src, ROW_TILE)], xbuf.at[slot, pl.ds(i * ROW_TILE, ROW_TILE)],
                                     gsem.at[slot])

    def scatter_row(slot, i):
        dst = pl.multiple_of(dst_s[slot, i], ROW_TILE)
        return pltpu.make_async_copy(ybuf.at[slot, pl.ds(i * ROW_TILE, ROW_TILE)], y4_hbm.at[pl.ds(dst, ROW_TILE)],
                                     ssem.at[slot])

    def wait_gathers(slot):
        pltpu.make_async_copy(h2_hbm.at[pl.ds(0, rows * ROW_TILE)], xbuf.at[slot], gsem.at[slot]).wait()

    def wait_scatters(slot):
        pltpu.make_async_copy(ybuf.at[slot], y4_hbm.at[pl.ds(0, rows * ROW_TILE)], ssem.at[slot]).wait()

    def issue_all(row_copy, slot):
        def body(i, c):
            row_copy(slot, i).start()
            return c
        lax.fori_loop(0, rows, body, 0, unroll=8)

    @pl.when(j == 0)
    def _():
        ybuf[...] = jnp.zeros_like(ybuf)
        pltpu.make_async_copy(ybuf.at[0], y4_hbm.at[pl.ds(spare0, rows * ROW_TILE)], ssem.at[0]).start()
        fill = pltpu.make_async_copy(ybuf.at[1], y4_hbm.at[pl.ds(spare0 + rows * ROW_TILE, rows * ROW_TILE)],
                                     ssem.at[1])
        fill.start()
        fill.wait()
        first = pltpu.make_async_copy(tok_hbm.at[0], tok_s.at[pl.ds(0, 1)], isem.at[0])
        first.start()
        first.wait()
        issue_all(gather_row, 0)
        for c in table_copies(1, nb, 1):
            c.start()

    def step(q, p):
        for c in table_copies(0, 0, p):
            c.wait()
        for c in table_copies(j + 2, j, q):
            c.start()
        wait_gathers(q)
        chunks = [xbuf[q, pl.ds(c, rows, stride=ROW_TILE), :] for c in range(ROW_TILE)]
        ssq = chunks[0] * chunks[0]
        for xc in chunks[1:]:
            ssq = ssq + xc * xc
        scale = lax.rsqrt(jnp.sum(ssq, axis=-1, keepdims=True) / (ROW_TILE * LANES) + RMS_EPS)
        hn = jnp.concatenate(
            [(xc * scale * gmoe_ref[:, c * LANES:(c + 1) * LANES]).astype(BF16) for c, xc in enumerate(chunks)],
            axis=-1)
        n_chunks = d_ff // FF_CHUNK
        per = rows // n_chunks
        for c in range(n_chunks):
            lo = c * FF_CHUNK
            glu = _dot(hn, wgu_ref[:, lo:lo + FF_CHUNK]) + bgu_ref[:, lo:lo + FF_CHUNK]
            lin = _dot(hn, wgu_ref[:, d_ff + lo:d_ff + lo + FF_CHUNK]) + bgu_ref[:, d_ff + lo:d_ff + lo + FF_CHUNK]
            glu = jnp.minimum(glu, SWIGLU_LIMIT)
            lin = jnp.clip(lin, -SWIGLU_LIMIT, SWIGLU_LIMIT)
            a = (lin + 1.0) * (glu * jax.nn.sigmoid(SWIGLU_ALPHA * glu))
            act[:, lo:lo + FF_CHUNK] = a.astype(BF16)
            for i in range(c * per, (c + 1) * per):
                gather_row(p, i).start()
                scatter_row(p, i).start()
        wait_scatters(q)
        y = _dot(act[...], wd_ref[...]) + bd_ref[...]
        for c in range(ROW_TILE):
            ybuf[q, pl.ds(c, rows, stride=ROW_TILE), :] = y[:, c * LANES:(c + 1) * LANES]

        @pl.when(j == n_used - 1)
        def _():
            for c in table_copies(0, 0, q):
                c.wait()
            issue_all(scatter_row, q)
            wait_scatters(q)
            wait_scatters(p)
            wait_gathers(p)

    @pl.when(jnp.logical_and(j < n_used, j % 2 == 0))
    def _():
        step(0, 1)

    @pl.when(jnp.logical_and(j < n_used, j % 2 == 1))
    def _():
        step(1, 0)


def _expert_call(block_e, n_used, tok_slot, dst_slot, h2, g_moe, w_gu, b_gu, w_d, b_d, out_rows):
    d = g_moe.shape[1]
    nb = block_e.shape[0]
    rows = EXPERT_ROWS
    d_ff = w_d.shape[1]
    assert d == ROW_TILE * LANES
    any_spec = pl.BlockSpec(memory_space=pl.ANY)
    grid_spec = pltpu.PrefetchScalarGridSpec(
        num_scalar_prefetch=2,
        grid=(nb,),
        in_specs=[
            any_spec, any_spec, any_spec,
            pl.BlockSpec((1, d), lambda j, be, nu: (0, 0)),
            pl.BlockSpec((None, d, 2 * d_ff), lambda j, be, nu: (be[j], 0, 0)),
            pl.BlockSpec((None, 1, 2 * d_ff), lambda j, be, nu: (be[j], 0, 0)),
            pl.BlockSpec((None, d_ff, d), lambda j, be, nu: (be[j], 0, 0)),
            pl.BlockSpec((None, 1, d), lambda j, be, nu: (be[j], 0, 0)),
        ],
        out_specs=any_spec,
        scratch_shapes=[
            pltpu.SMEM((2, rows), jnp.int32), pltpu.SMEM((2, rows), jnp.int32),
            pltpu.VMEM((2, rows * ROW_TILE, LANES), F32), pltpu.VMEM((2, rows * ROW_TILE, LANES), F32),
            pltpu.VMEM((rows, d_ff), BF16),
            pltpu.SemaphoreType.DMA((2,)), pltpu.SemaphoreType.DMA((2,)), pltpu.SemaphoreType.DMA((2,)),
        ],
    )
    return pl.pallas_call(
        _expert_kernel,
        grid_spec=grid_spec,
        out_shape=jax.ShapeDtypeStruct((out_rows * ROW_TILE, LANES), F32),
        compiler_params=pltpu.CompilerParams(dimension_semantics=("arbitrary",), vmem_limit_bytes=VMEM_LIMIT),
        name="expert_ffn",
    )(block_e, n_used, tok_slot, dst_slot, h2, g_moe, w_gu, b_gu, w_d, b_d)


def _combine_kernel(h2_ref, y0_ref, y1_ref, y2_ref, y3_ref, gate_ref, g_ref, o_ref):
    ts = o_ref.shape[0]
    gts = gate_ref[...]
    planes = (y0_ref, y1_ref, y2_ref, y3_ref)
    outs = []
    for c in range(ROW_TILE):
        chunk = pl.ds(c, ts, stride=ROW_TILE)
        acc = planes[0][chunk, :] * gts[:, 0:1]
        for k in range(1, TOP_K):
            acc = acc + planes[k][chunk, :] * gts[:, k:k + 1]
        outs.append(h2_ref[chunk, :] + acc)
    ssq = outs[0] * outs[0]
    for oc in outs[1:]:
        ssq = ssq + oc * oc
    scale = lax.rsqrt(jnp.sum(ssq, axis=-1, keepdims=True) / (ROW_TILE * LANES) + RMS_EPS)
    for c, oc in enumerate(outs):
        o_ref[:, c * LANES:(c + 1) * LANES] = oc * scale * g_ref[:, c * LANES:(c + 1) * LANES]


def _combine_call(h2, y4, gate, g_final):
    t = gate.shape[0]
    d = g_final.shape[1]
    ts = SEQ_TILE
    nt = t // ts
    blk = (ts * ROW_TILE, LANES)
    plane = lambda k: pl.BlockSpec(blk, lambda i: (k * nt + i, 0))
    return pl.pallas_call(
        _combine_kernel,
        grid=(nt,),
        in_specs=[pl.BlockSpec(blk, lambda i: (i, 0)), plane(0), plane(1), plane(2), plane(3),
                  pl.BlockSpec((ts, LANES), lambda i: (i, 0)), pl.BlockSpec((1, d), lambda i: (0, 0))],
        out_specs=pl.BlockSpec((ts, d), lambda i: (i, 0)),
        out_shape=jax.ShapeDtypeStruct((t, d), F32),
        compiler_params=pltpu.CompilerParams(dimension_semantics=("arbitrary",), vmem_limit_bytes=VMEM_LIMIT),
        name="combine_norm",
    )(h2, y4, y4, y4, y4, gate, g_final)


def _slot_tables(eidx, counts, t):
    rows = EXPERT_ROWS
    n_real = TOP_K * t
    assert n_real <= KEY_PAD_FLAG and rows <= KEY_PAD_FLAG
    n_slots = n_real + N_EXPERTS * rows
    nb = n_slots // rows
    padded = ((counts + rows - 1) // rows) * rows
    pad = padded - counts
    pstarts = jnp.cumsum(padded) - padded
    flat = jnp.arange(n_real, dtype=jnp.int32).reshape(t, TOP_K)
    real_keys = (eidx << KEY_EXPERT_SHIFT) | flat
    i = jnp.arange(rows, dtype=jnp.int32)[None, :]
    e = jnp.arange(N_EXPERTS, dtype=jnp.int32)[:, None]
    pad_keys = jnp.where(i < pad[:, None], (e << KEY_EXPERT_SHIFT) | KEY_PAD_FLAG | i, jnp.iinfo(jnp.int32).max)
    keys = lax.sort(jnp.concatenate([real_keys.reshape(-1), pad_keys.reshape(-1)]))
    real = (keys & KEY_PAD_FLAG) == 0
    a = keys & (KEY_PAD_FLAG - 1)
    slot = jnp.arange(n_slots, dtype=jnp.int32)
    tok_slot = jnp.where(real, a // TOP_K, 0) * ROW_TILE
    dst_slot = jnp.where(real, (a % TOP_K) * t + a // TOP_K, n_real + slot % (2 * rows))
    tok_slot = jnp.concatenate([tok_slot, jnp.zeros((rows,), jnp.int32)]).reshape(nb + 1, 1, rows)
    dst_slot = jnp.concatenate([dst_slot, n_real + 2 * rows + jnp.arange(rows, dtype=jnp.int32)]) * ROW_TILE
    dst_slot = dst_slot.reshape(nb + 1, 1, rows)
    n_used = (jnp.sum(padded) // rows).astype(jnp.int32)
    starts = jnp.arange(nb, dtype=jnp.int32) * rows
    block_e = jnp.sum((pstarts[None, :] <= starts[:, None]).astype(jnp.int32), axis=1) - 1
    block_e = jnp.where(jnp.arange(nb) < n_used, block_e, block_e[jnp.maximum(n_used - 1, 0)])
    return block_e, n_used.reshape(1), tok_slot, dst_slot, n_real + 3 * rows


def kernel(x, mem, g_mix, w_in, w_pool, pool_scale, conv_w, w_out, g_xattn, g_mem, w_q, w_k, w_v, w_o,
           g_moe, w_router, b_router, w_gate_up, b_gate_up, w_down, b_down, g_final):
    b, s, d = x.shape
    t = b * s
    assert g_mix.shape[0] == 1, "the combine kernel fuses the final norm, so exactly one layer is supported"
    assert s % SEQ_TILE == 0 and d == ROW_TILE * LANES
    row = lambda a: a[0].reshape(1, -1)
    kmem, vmem = _kv_call(mem, row(g_mem), w_k[0].astype(BF16), w_v[0].astype(BF16))
    h2, eidx, gate, counts = _token_call(
        x, kmem, vmem, row(g_mix), w_in[0].astype(BF16), w_pool[0].astype(BF16), row(pool_scale), conv_w[0],
        w_out[0].astype(BF16), row(g_xattn), w_q[0].astype(BF16), w_o[0].astype(BF16), row(g_moe),
        w_router[0].astype(BF16), row(b_router))
    block_e, n_used, tok_slot, dst_slot, out_rows = _slot_tables(eidx[:, :TOP_K], counts[0].astype(jnp.int32), t)
    y4 = _expert_call(block_e, n_used, tok_slot, dst_slot, h2, row(g_moe),
                      w_gate_up[0].astype(BF16), b_gate_up[0][:, None, :],
                      w_down[0].astype(BF16), b_down[0][:, None, :], out_rows)
    return _combine_call(h2, y4, gate, g_final.reshape(1, -1)).reshape(b, s, d)
```

```python
import jax
import jax.numpy as jnp
from jax import lax
from jax.experimental import pallas as pl
from jax.experimental.pallas import tpu as pltpu

POOL_WINDOWS = (2, 4, 8, 16)
POOL_GROUP_W = 128
POOL_W = 512
CONV_W = 512
N_XHEADS = 4
N_EXPERTS = 32
TOP_K = 4
SWIGLU_LIMIT = 7.0
SWIGLU_ALPHA = 1.702
RMS_EPS = 1e-5

LANES = 128
ROW_TILE = 8
SEQ_TILE = 512
EXPERT_ROWS = 512
FF_CHUNK = 256
CAST_ROWS = 64
HALO = 16
VMEM_LIMIT = 56 * 1024 * 1024

KEY_ID_BITS = 18
KEY_PAD_FLAG = 1 << KEY_ID_BITS
KEY_EXPERT_SHIFT = KEY_ID_BITS + 1

F32 = jnp.float32
BF16 = jnp.bfloat16


def _rms(x, g):
    ms = jnp.mean(x * x, axis=-1, keepdims=True)
    return x * lax.rsqrt(ms + RMS_EPS) * g


def _dot(a, b):
    return jnp.dot(a, b, preferred_element_type=F32)


def _kv_kernel(mem_ref, g_ref, wk_ref, wv_ref, k_ref, v_ref):
    mn = _rms(mem_ref[...], g_ref[...]).astype(BF16)
    k_ref[...] = _dot(mn, wk_ref[...]).astype(BF16)
    v_ref[...] = _dot(mn, wv_ref[...]).astype(BF16)


def _kv_call(mem, g_mem, w_k, w_v):
    b, m, d = mem.shape
    full = lambda shape: pl.BlockSpec(shape, lambda i: (0,) * len(shape))
    return pl.pallas_call(
        _kv_kernel,
        grid=(b,),
        in_specs=[pl.BlockSpec((None, m, d), lambda i: (i, 0, 0)), full((1, d)), full((d, d)), full((d, d))],
        out_specs=[pl.BlockSpec((None, m, d), lambda i: (i, 0, 0))] * 2,
        out_shape=[jax.ShapeDtypeStruct((b, m, d), BF16)] * 2,
        compiler_params=pltpu.CompilerParams(dimension_semantics=("arbitrary",), vmem_limit_bytes=VMEM_LIMIT),
        name="kv_proj",
    )(mem, g_mem, w_k, w_v)


def _token_kernel(x_ref, gmix_ref, win_ref, wpool_ref, pscale_ref, convw_ref, wout_ref,
                  gx_ref, k_ref, v_ref, wq_ref, wo_ref, gmoe_ref, wr_ref, br_ref,
                  h2_ref, eidx_ref, gate_ref, cnt_ref,
                  uhalo, zhalo, carry):
    b = pl.program_id(0)
    s = pl.program_id(1)
    ts = x_ref.shape[0]

    @pl.when(s == 0)
    def _():
        uhalo[...] = jnp.zeros_like(uhalo)
        zhalo[...] = jnp.zeros_like(zhalo)

    @pl.when(jnp.logical_and(b == 0, s == 0))
    def _():
        carry[...] = jnp.zeros_like(carry)

    x = x_ref[...]
    hn = _rms(x, gmix_ref[...]).astype(BF16)
    proj = _dot(hn, win_ref[...])
    u = proj[:, :POOL_W]
    gb = proj[:, POOL_W:POOL_W + CONV_W]
    gc = proj[:, POOL_W + CONV_W:POOL_W + 2 * CONV_W]
    v = proj[:, POOL_W + 2 * CONV_W:]

    pos = (s * ts + 1 + lax.broadcasted_iota(jnp.int32, (ts, 1), 0)).astype(F32)
    ucat = jnp.concatenate([uhalo[...], u], axis=0)
    mix = None
    for g, w in enumerate(POOL_WINDOWS):
        cols = slice(g * POOL_GROUP_W, (g + 1) * POOL_GROUP_W)
        win = ucat[:, cols]
        span = 1
        while span < w:
            win = win + pltpu.roll(win, span, 0)
            span *= 2
        inv_cnt = 1.0 / jnp.minimum(pos, float(w))
        dg = win[HALO:, :] * inv_cnt - u[:, cols]
        yg = _dot(dg.astype(BF16), wpool_ref[g]) * pscale_ref[:, cols]
        part = _dot(yg.astype(BF16), wout_ref[cols, :])
        mix = part if mix is None else mix + part
    uhalo[...] = u[ts - HALO:, :]

    z = gc * v
    zcat = jnp.concatenate([zhalo[...], z], axis=0)
    z1 = pltpu.roll(zcat, 1, 0)[8:, :]
    z2 = pltpu.roll(zcat, 2, 0)[8:, :]
    zc = convw_ref[0:1, :] * z2 + convw_ref[1:2, :] * z1 + convw_ref[2:3, :] * z
    zhalo[...] = z[ts - 8:, :]
    y_conv = gb * zc
    mix = mix + _dot(y_conv.astype(BF16), wout_ref[POOL_W:, :])
    h1 = x + mix

    hq = _rms(h1, gx_ref[...]).astype(BF16)
    q = _dot(hq, wq_ref[...]).astype(BF16)
    hd = q.shape[1] // N_XHEADS
    attn = None
    for h in range(N_XHEADS):
        cols = slice(h * hd, (h + 1) * hd)
        sc = lax.dot_general(q[:, cols], k_ref[:, cols], (((1,), (1,)), ((), ())),
                             preferred_element_type=F32) * (hd ** -0.5)
        sc = sc - jnp.max(sc, axis=-1, keepdims=True)
        p = jnp.exp(sc)
        p = p / jnp.sum(p, axis=-1, keepdims=True)
        oh = _dot(p.astype(BF16), v_ref[:, cols])
        part = _dot(oh.astype(BF16), wo_ref[cols, :])
        attn = part if attn is None else attn + part
    h2 = h1 + attn
    for c in range(ROW_TILE):
        h2_ref[pl.ds(c, ts, stride=ROW_TILE), :] = h2[:, c * LANES:(c + 1) * LANES]

    hm = _rms(h2, gmoe_ref[...]).astype(BF16)
    logits = _dot(hm, wr_ref[...]) + br_ref[...]
    ne = logits.shape[1]
    lane = lax.broadcasted_iota(jnp.int32, (ts, ne), 1).astype(F32)
    sel = jnp.zeros((ts, ne), F32)
    vals, idxs = [], []
    for _ in range(TOP_K):
        m = jnp.max(logits, axis=-1, keepdims=True)
        idx = jnp.min(jnp.where(logits == m, lane, float(ne)), axis=-1, keepdims=True)
        hot = lane == idx
        logits = jnp.where(hot, -jnp.inf, logits)
        sel = jnp.where(hot, 1.0, sel)
        vals.append(m)
        idxs.append(idx)
    exps = [jnp.exp(vk - vals[0]) for vk in vals]
    denom = exps[0]
    for ek in exps[1:]:
        denom = denom + ek
    gates = [ek / denom for ek in exps]

    carry[...] = carry[...] + jnp.sum(sel, axis=0, keepdims=True)
    cnt_ref[...] = carry[...]

    out_lane = lax.broadcasted_iota(jnp.int32, (ts, LANES), 1)

    def spread(cols_):
        acc = jnp.zeros((ts, LANES), F32)
        for kk, c in enumerate(cols_):
            acc = jnp.where(out_lane == kk, c, acc)
        return acc

    eidx_ref[...] = spread(idxs).astype(jnp.int32)
    gate_ref[...] = spread(gates)


def _token_call(x, kmem, vmem, g_mix, w_in, w_pool, pool_scale, conv_w, w_out, g_xattn, w_q, w_o,
                g_moe, w_router, b_router):
    b, s, d = x.shape
    ts = SEQ_TILE
    ns = s // ts
    t = b * s
    m = kmem.shape[1]
    ne = w_router.shape[1]

    def full(shape):
        return pl.BlockSpec(shape, lambda i, j: (0,) * len(shape))

    tok_out = pl.BlockSpec((ts, LANES), lambda i, j: (i * ns + j, 0))
    return pl.pallas_call(
        _token_kernel,
        grid=(b, ns),
        in_specs=[
            pl.BlockSpec((None, ts, d), lambda i, j: (i, j, 0)),
            full((1, d)), full(w_in.shape), full(w_pool.shape), full((1, POOL_W)), full(conv_w.shape),
            full(w_out.shape), full((1, d)),
            pl.BlockSpec((None, m, d), lambda i, j: (i, 0, 0)),
            pl.BlockSpec((None, m, d), lambda i, j: (i, 0, 0)),
            full(w_q.shape), full(w_o.shape), full((1, d)), full(w_router.shape), full((1, ne)),
        ],
        out_specs=[
            pl.BlockSpec((ts * ROW_TILE, LANES), lambda i, j: (i * ns + j, 0)),
            tok_out, tok_out,
            pl.BlockSpec((1, ne), lambda i, j: (0, 0)),
        ],
        out_shape=[
            jax.ShapeDtypeStruct((t * ROW_TILE, LANES), F32),
            jax.ShapeDtypeStruct((t, LANES), jnp.int32),
            jax.ShapeDtypeStruct((t, LANES), F32),
            jax.ShapeDtypeStruct((1, ne), F32),
        ],
        scratch_shapes=[pltpu.VMEM((HALO, POOL_W), F32), pltpu.VMEM((8, CONV_W), F32), pltpu.VMEM((1, ne), F32)],
        compiler_params=pltpu.CompilerParams(dimension_semantics=("arbitrary", "arbitrary"),
                                             vmem_limit_bytes=VMEM_LIMIT),
        name="token_mix_attn_route",
    )(x, g_mix, w_in, w_pool, pool_scale, conv_w, w_out, g_xattn, kmem, vmem, w_q, w_o, g_moe, w_router, b_router)


def _expert_kernel(be_ref, nused_ref,
                   tok_hbm, dst_hbm, h2_hbm, gmoe_ref, wgu_ref, bgu_ref, wd_ref, bd_ref,
                   y4_hbm,
                   tok_s, dst_s, xbuf, ybuf, act, wgu_bf, wd_bf, gsem, ssem, tsem, dsem):
    j = pl.program_id(0)
    nb = pl.num_programs(0)
    rows = act.shape[0]
    d_ff = wd_ref.shape[0]
    n_used = nused_ref[0]
    block_rows = rows * ROW_TILE
    spare0 = y4_hbm.shape[0] - 3 * block_rows

    new_expert = jnp.logical_or(j == 0, be_ref[j] != be_ref[jnp.maximum(j - 1, 0)])

    @pl.when(jnp.logical_and(j < n_used, new_expert))
    def _():
        def cast(r, c):
            rs = pl.ds(pl.multiple_of(r * CAST_ROWS, CAST_ROWS), CAST_ROWS)
            wgu_bf[rs, :] = wgu_ref[rs, :].astype(BF16)
            wd_bf[rs, :] = wd_ref[rs, :].astype(BF16)
            return c
        lax.fori_loop(0, d_ff // CAST_ROWS, cast, 0)

    def tok_table(blk, slot):
        return pltpu.make_async_copy(tok_hbm.at[blk], tok_s.at[pl.ds(slot, 1)], tsem.at[slot])

    def dst_table(blk, slot):
        return pltpu.make_async_copy(dst_hbm.at[blk], dst_s.at[pl.ds(slot, 1)], dsem.at[slot])

    def gather_row(slot, i):
        src = pl.multiple_of(tok_s[slot, i], ROW_TILE)
        return pltpu.make_async_copy(h2_hbm.at[pl.ds(src, ROW_TILE)], xbuf.at[slot, pl.ds(i * ROW_TILE, ROW_TILE)],
                                     gsem.at[slot])

    def scatter_row(slot, i):
        dst = pl.multiple_of(dst_s[slot, i], ROW_TILE)
        return pltpu.make_async_copy(ybuf.at[slot, pl.ds(i * ROW_TILE, ROW_TILE)], y4_hbm.at[pl.ds(dst, ROW_TILE)],
                                     ssem.at[slot])

    def wait_gathers(slot):
        pltpu.make_async_copy(h2_hbm.at[pl.ds(0, block_rows)], xbuf.at[slot], gsem.at[slot]).wait()

    def wait_scatters(slot):
        pltpu.make_async_copy(ybuf.at[slot], y4_hbm.at[pl.ds(0, block_rows)], ssem.at[slot]).wait()

    def issue_all(row_copy, slot):
        def body(i, c):
            row_copy(slot, i).start()
            return c
        lax.fori_loop(0, rows, body, 0, unroll=8)

    @pl.when(j == 0)
    def _():
        ybuf[...] = jnp.zeros_like(ybuf)
        pltpu.make_async_copy(ybuf.at[0], y4_hbm.at[pl.ds(spare0, block_rows)], ssem.at[0]).start()
        fill = pltpu.make_async_copy(ybuf.at[1], y4_hbm.at[pl.ds(spare0 + block_rows, block_rows)], ssem.at[1])
        fill.start()
        fill.wait()
        first = tok_table(0, 0)
        first.start()
        first.wait()
        issue_all(gather_row, 0)
        tok_table(1, 1).start()
        dst_table(nb, 1).start()

    def step(q, p):
        tok_table(0, p).wait()
        dst_table(0, p).wait()
        tok_table(j + 2, q).start()
        dst_table(j, q).start()
        wait_gathers(q)
        chunks = [xbuf[q, pl.ds(c, rows, stride=ROW_TILE), :] for c in range(ROW_TILE)]
        ssq = chunks[0] * chunks[0]
        for xc in chunks[1:]:
            ssq = ssq + xc * xc
        scale = lax.rsqrt(jnp.sum(ssq, axis=-1, keepdims=True) / (ROW_TILE * LANES) + RMS_EPS)
        hn = jnp.concatenate(
            [(xc * scale * gmoe_ref[:, c * LANES:(c + 1) * LANES]).astype(BF16) for c, xc in enumerate(chunks)],
            axis=-1)
        n_chunks = d_ff // FF_CHUNK
        per = 2 * rows // n_chunks
        for c in range(n_chunks):
            lo = c * FF_CHUNK
            glu = _dot(hn, wgu_bf[:, lo:lo + FF_CHUNK]) + bgu_ref[:, lo:lo + FF_CHUNK]
            lin = _dot(hn, wgu_bf[:, d_ff + lo:d_ff + lo + FF_CHUNK]) + bgu_ref[:, d_ff + lo:d_ff + lo + FF_CHUNK]
            glu = jnp.minimum(glu, SWIGLU_LIMIT)
            lin = jnp.clip(lin, -SWIGLU_LIMIT, SWIGLU_LIMIT)
            a = (lin + 1.0) * (glu * jax.nn.sigmoid(SWIGLU_ALPHA * glu))
            act[:, lo:lo + FF_CHUNK] = a.astype(BF16)
            if 2 * c < n_chunks:
                for i in range(c * per, (c + 1) * per):
                    gather_row(p, i).start()
            else:
                for i in range(c * per - rows, (c + 1) * per - rows):
                    scatter_row(p, i).start(priority=1)
        wait_scatters(q)
        y = _dot(act[...], wd_bf[...]) + bd_ref[...]
        for c in range(ROW_TILE):
            ybuf[q, pl.ds(c, rows, stride=ROW_TILE), :] = y[:, c * LANES:(c + 1) * LANES]

        @pl.when(j == n_used - 1)
        def _():
            tok_table(0, q).wait()
            dst_table(0, q).wait()
            issue_all(scatter_row, q)
            wait_scatters(q)
            wait_scatters(p)
            wait_gathers(p)

    @pl.when(jnp.logical_and(j < n_used, j % 2 == 0))
    def _():
        step(0, 1)

    @pl.when(jnp.logical_and(j < n_used, j % 2 == 1))
    def _():
        step(1, 0)


def _expert_call(block_e, n_used, tok_slot, dst_slot, h2, g_moe, w_gu, b_gu, w_d, b_d, out_rows):
    d = g_moe.shape[1]
    nb = block_e.shape[0]
    rows = EXPERT_ROWS
    d_ff = w_d.shape[1]
    assert d == ROW_TILE * LANES
    any_spec = pl.BlockSpec(memory_space=pl.ANY)
    grid_spec = pltpu.PrefetchScalarGridSpec(
        num_scalar_prefetch=2,
        grid=(nb,),
        in_specs=[
            any_spec, any_spec, any_spec,
            pl.BlockSpec((1, d), lambda j, be, nu: (0, 0)),
            pl.BlockSpec((None, d, 2 * d_ff), lambda j, be, nu: (be[j], 0, 0)),
            pl.BlockSpec((None, 1, 2 * d_ff), lambda j, be, nu: (be[j], 0, 0)),
            pl.BlockSpec((None, d_ff, d), lambda j, be, nu: (be[j], 0, 0)),
            pl.BlockSpec((None, 1, d), lambda j, be, nu: (be[j], 0, 0)),
        ],
        out_specs=any_spec,
        scratch_shapes=[
            pltpu.SMEM((2, rows), jnp.int32), pltpu.SMEM((2, rows), jnp.int32),
            pltpu.VMEM((2, rows * ROW_TILE, LANES), F32), pltpu.VMEM((2, rows * ROW_TILE, LANES), F32),
            pltpu.VMEM((rows, d_ff), BF16), pltpu.VMEM((d, 2 * d_ff), BF16), pltpu.VMEM((d_ff, d), BF16),
            pltpu.SemaphoreType.DMA((2,)), pltpu.SemaphoreType.DMA((2,)), pltpu.SemaphoreType.DMA((2,)),
            pltpu.SemaphoreType.DMA((2,)),
        ],
    )
    return pl.pallas_call(
        _expert_kernel,
        grid_spec=grid_spec,
        out_shape=jax.ShapeDtypeStruct((out_rows * ROW_TILE, LANES), F32),
        compiler_params=pltpu.CompilerParams(dimension_semantics=("arbitrary",), vmem_limit_bytes=VMEM_LIMIT),
        name="expert_ffn",
    )(block_e, n_used, tok_slot, dst_slot, h2, g_moe, w_gu, b_gu, w_d, b_d)


def _combine_kernel(h2_ref, y0_ref, y1_ref, y2_ref, y3_ref, gate_ref, g_ref, o_ref):
    ts = o_ref.shape[0]
    gts = gate_ref[...]
    planes = (y0_ref, y1_ref, y2_ref, y3_ref)
    outs = []
    for c in range(ROW_TILE):
        chunk = pl.ds(c, ts, stride=ROW_TILE)
        acc = planes[0][chunk, :] * gts[:, 0:1]
        for k in range(1, TOP_K):
            acc = acc + planes[k][chunk, :] * gts[:, k:k + 1]
        outs.append(h2_ref[chunk, :] + acc)
    ssq = outs[0] * outs[0]
    for oc in outs[1:]:
        ssq = ssq + oc * oc
    scale = lax.rsqrt(jnp.sum(ssq, axis=-1, keepdims=True) / (ROW_TILE * LANES) + RMS_EPS)
    for c, oc in enumerate(outs):
        o_ref[:, c * LANES:(c + 1) * LANES] = oc * scale * g_ref[:, c * LANES:(c + 1) * LANES]


def _combine_call(h2, y4, gate, g_final):
    t = gate.shape[0]
    d = g_final.shape[1]
    ts = SEQ_TILE
    nt = t // ts
    blk = (ts * ROW_TILE, LANES)
    plane = lambda k: pl.BlockSpec(blk, lambda i: (k * nt + i, 0))
    return pl.pallas_call(
        _combine_kernel,
        grid=(nt,),
        in_specs=[pl.BlockSpec(blk, lambda i: (i, 0)), plane(0), plane(1), plane(2), plane(3),
                  pl.BlockSpec((ts, LANES), lambda i: (i, 0)), pl.BlockSpec((1, d), lambda i: (0, 0))],
        out_specs=pl.BlockSpec((ts, d), lambda i: (i, 0)),
        out_shape=jax.ShapeDtypeStruct((t, d), F32),
        compiler_params=pltpu.CompilerParams(dimension_semantics=("arbitrary",), vmem_limit_bytes=VMEM_LIMIT),
        name="combine_norm",
    )(h2, y4, y4, y4, y4, gate, g_final)


def _slot_tables(eidx, counts, t):
    rows = EXPERT_ROWS
    n_real = TOP_K * t
    assert n_real <= KEY_PAD_FLAG and rows <= KEY_PAD_FLAG
    n_slots = n_real + N_EXPERTS * rows
    nb = n_slots // rows
    padded = ((counts + rows - 1) // rows) * rows
    pad = padded - counts
    pstarts = jnp.cumsum(padded) - padded
    flat = jnp.arange(n_real, dtype=jnp.int32).reshape(t, TOP_K)
    real_keys = (eidx << KEY_EXPERT_SHIFT) | flat
    i = jnp.arange(rows, dtype=jnp.int32)[None, :]
    e = jnp.arange(N_EXPERTS, dtype=jnp.int32)[:, None]
    pad_keys = jnp.where(i < pad[:, None], (e << KEY_EXPERT_SHIFT) | KEY_PAD_FLAG | i, jnp.iinfo(jnp.int32).max)
    keys = lax.sort(jnp.concatenate([real_keys.reshape(-1), pad_keys.reshape(-1)]))
    real = (keys & KEY_PAD_FLAG) == 0
    a = keys & (KEY_PAD_FLAG - 1)
    slot = jnp.arange(n_slots, dtype=jnp.int32)
    tok_slot = jnp.where(real, a // TOP_K, 0) * ROW_TILE
    dst_slot = jnp.where(real, (a % TOP_K) * t + a // TOP_K, n_real + slot % (2 * rows))
    tok_slot = jnp.concatenate([tok_slot, jnp.zeros((rows,), jnp.int32)]).reshape(nb + 1, 1, rows)
    dst_slot = jnp.concatenate([dst_slot, n_real + 2 * rows + jnp.arange(rows, dtype=jnp.int32)]) * ROW_TILE
    dst_slot = dst_slot.reshape(nb + 1, 1, rows)
    n_used = (jnp.sum(padded) // rows).astype(jnp.int32)
    starts = jnp.arange(nb, dtype=jnp.int32) * rows
    block_e = jnp.sum((pstarts[None, :] <= starts[:, None]).astype(jnp.int32), axis=1) - 1
    block_e = jnp.where(jnp.arange(nb) < n_used, block_e, block_e[jnp.maximum(n_used - 1, 0)])
    return block_e, n_used.reshape(1), tok_slot, dst_slot, n_real + 3 * rows


def kernel(x, mem, g_mix, w_in, w_pool, pool_scale, conv_w, w_out, g_xattn, g_mem, w_q, w_k, w_v, w_o,
           g_moe, w_router, b_router, w_gate_up, b_gate_up, w_down, b_down, g_final):
    b, s, d = x.shape
    t = b * s
    assert g_mix.shape[0] == 1, "the combine kernel fuses the final norm, so exactly one layer is supported"
    assert s % SEQ_TILE == 0 and d == ROW_TILE * LANES
    row = lambda a: a[0].reshape(1, -1)
    kmem, vmem = _kv_call(mem, row(g_mem), w_k[0].astype(BF16), w_v[0].astype(BF16))
    h2, eidx, gate, counts = _token_call(
        x, kmem, vmem, row(g_mix), w_in[0].astype(BF16), w_pool[0].astype(BF16), row(pool_scale), conv_w[0],
        w_out[0].astype(BF16), row(g_xattn), w_q[0].astype(BF16), w_o[0].astype(BF16), row(g_moe),
        w_router[0].astype(BF16), row(b_router))
    block_e, n_used, tok_slot, dst_slot, out_rows = _slot_tables(eidx[:, :TOP_K], counts[0].astype(jnp.int32), t)
    y4 = _expert_call(block_e, n_used, tok_slot, dst_slot, h2, row(g_moe),
                      w_gate_up[0], b_gate_up[0][:, None, :], w_down[0], b_down[0][:, None, :], out_rows)
    return _combine_call(h2, y4, gate, g_final.reshape(1, -1)).reshape(b, s, d)
```

```python
import jax
import jax.numpy as jnp
from jax import lax
from jax.experimental import pallas as pl
from jax.experimental.pallas import tpu as pltpu

POOL_WINDOWS = (2, 4, 8, 16)
POOL_GROUP_W = 128
POOL_W = 512
CONV_W = 512
N_XHEADS = 4
N_EXPERTS = 32
TOP_K = 4
SWIGLU_LIMIT = 7.0
SWIGLU_ALPHA = 1.702
RMS_EPS = 1e-5

LANES = 128
ROW_TILE = 8
SEQ_TILE = 512
EXPERT_ROWS = 512
FF_CHUNK = 256
CAST_ROWS = 64
HALO = 16
VMEM_LIMIT = 56 * 1024 * 1024

KEY_ID_BITS = 18
KEY_PAD_FLAG = 1 << KEY_ID_BITS
KEY_EXPERT_SHIFT = KEY_ID_BITS + 1

F32 = jnp.float32
BF16 = jnp.bfloat16


def _rms(x, g):
    ms = jnp.mean(x * x, axis=-1, keepdims=True)
    return x * lax.rsqrt(ms + RMS_EPS) * g


def _dot(a, b):
    return jnp.dot(a, b, preferred_element_type=F32)


def _kv_kernel(mem_ref, g_ref, wk_ref, wv_ref, k_ref, v_ref):
    mn = _rms(mem_ref[...], g_ref[...]).astype(BF16)
    k_ref[...] = _dot(mn, wk_ref[...]).astype(BF16)
    v_ref[...] = _dot(mn, wv_ref[...]).astype(BF16)


def _kv_call(mem, g_mem, w_k, w_v):
    b, m, d = mem.shape
    full = lambda shape: pl.BlockSpec(shape, lambda i: (0,) * len(shape))
    return pl.pallas_call(
        _kv_kernel,
        grid=(b,),
        in_specs=[pl.BlockSpec((None, m, d), lambda i: (i, 0, 0)), full((1, d)), full((d, d)), full((d, d))],
        out_specs=[pl.BlockSpec((None, m, d), lambda i: (i, 0, 0))] * 2,
        out_shape=[jax.ShapeDtypeStruct((b, m, d), BF16)] * 2,
        compiler_params=pltpu.CompilerParams(dimension_semantics=("arbitrary",), vmem_limit_bytes=VMEM_LIMIT),
        name="kv_proj",
    )(mem, g_mem, w_k, w_v)


def _token_kernel(x_ref, gmix_ref, win_ref, wpool_ref, pscale_ref, convw_ref, wout_ref,
                  gx_ref, k_ref, v_ref, wq_ref, wo_ref, gmoe_ref, wr_ref, br_ref,
                  h2_ref, eidx_ref, gate_ref, cnt_ref,
                  uhalo, zhalo, carry):
    b = pl.program_id(0)
    s = pl.program_id(1)
    ts = x_ref.shape[0]

    @pl.when(s == 0)
    def _():
        uhalo[...] = jnp.zeros_like(uhalo)
        zhalo[...] = jnp.zeros_like(zhalo)

    @pl.when(jnp.logical_and(b == 0, s == 0))
    def _():
        carry[...] = jnp.zeros_like(carry)

    x = x_ref[...]
    hn = _rms(x, gmix_ref[...]).astype(BF16)
    proj = _dot(hn, win_ref[...])
    u = proj[:, :POOL_W]
    gb = proj[:, POOL_W:POOL_W + CONV_W]
    gc = proj[:, POOL_W + CONV_W:POOL_W + 2 * CONV_W]
    v = proj[:, POOL_W + 2 * CONV_W:]

    pos = (s * ts + 1 + lax.broadcasted_iota(jnp.int32, (ts, 1), 0)).astype(F32)
    ucat = jnp.concatenate([uhalo[...], u], axis=0)
    mixed = []
    for g, w in enumerate(POOL_WINDOWS):
        cols = slice(g * POOL_GROUP_W, (g + 1) * POOL_GROUP_W)
        win = ucat[:, cols]
        span = 1
        while span < w:
            win = win + pltpu.roll(win, span, 0)
            span *= 2
        inv_cnt = 1.0 / jnp.minimum(pos, float(w))
        dg = win[HALO:, :] * inv_cnt - u[:, cols]
        yg = _dot(dg.astype(BF16), wpool_ref[g]) * pscale_ref[:, cols]
        mixed.append(yg.astype(BF16))
    uhalo[...] = u[ts - HALO:, :]

    z = gc * v
    zcat = jnp.concatenate([zhalo[...], z], axis=0)
    z1 = pltpu.roll(zcat, 1, 0)[8:, :]
    z2 = pltpu.roll(zcat, 2, 0)[8:, :]
    zc = convw_ref[0:1, :] * z2 + convw_ref[1:2, :] * z1 + convw_ref[2:3, :] * z
    zhalo[...] = z[ts - 8:, :]
    mixed.append((gb * zc).astype(BF16))
    h1 = x + _dot(jnp.concatenate(mixed, axis=-1), wout_ref[...])

    hq = _rms(h1, gx_ref[...]).astype(BF16)
    q = _dot(hq, wq_ref[...]).astype(BF16)
    hd = q.shape[1] // N_XHEADS
    attn = None
    for h in range(N_XHEADS):
        cols = slice(h * hd, (h + 1) * hd)
        sc = lax.dot_general(q[:, cols], k_ref[:, cols], (((1,), (1,)), ((), ())),
                             preferred_element_type=F32) * (hd ** -0.5)
        sc = sc - jnp.max(sc, axis=-1, keepdims=True)
        p = jnp.exp(sc)
        p = p / jnp.sum(p, axis=-1, keepdims=True)
        oh = _dot(p.astype(BF16), v_ref[:, cols])
        part = _dot(oh.astype(BF16), wo_ref[cols, :])
        attn = part if attn is None else attn + part
    h2 = h1 + attn
    for c in range(ROW_TILE):
        h2_ref[pl.ds(c, ts, stride=ROW_TILE), :] = h2[:, c * LANES:(c + 1) * LANES]

    hm = _rms(h2, gmoe_ref[...]).astype(BF16)
    logits = _dot(hm, wr_ref[...]) + br_ref[...]
    ne = logits.shape[1]
    lane = lax.broadcasted_iota(jnp.int32, (ts, ne), 1).astype(F32)
    sel = jnp.zeros((ts, ne), F32)
    vals, idxs = [], []
    for _ in range(TOP_K):
        m = jnp.max(logits, axis=-1, keepdims=True)
        idx = jnp.min(jnp.where(logits == m, lane, float(ne)), axis=-1, keepdims=True)
        hot = lane == idx
        logits = jnp.where(hot, -jnp.inf, logits)
        sel = jnp.where(hot, 1.0, sel)
        vals.append(m)
        idxs.append(idx)
    exps = [jnp.exp(vk - vals[0]) for vk in vals]
    denom = exps[0]
    for ek in exps[1:]:
        denom = denom + ek
    gates = [ek / denom for ek in exps]

    carry[...] = carry[...] + jnp.sum(sel, axis=0, keepdims=True)
    cnt_ref[...] = carry[...]

    out_lane = lax.broadcasted_iota(jnp.int32, (ts, LANES), 1)

    def spread(cols_):
        acc = jnp.zeros((ts, LANES), F32)
        for kk, c in enumerate(cols_):
            acc = jnp.where(out_lane == kk, c, acc)
        return acc

    eidx_ref[...] = spread(idxs).astype(jnp.int32)
    gate_ref[...] = spread(gates)


def _token_call(x, kmem, vmem, g_mix, w_in, w_pool, pool_scale, conv_w, w_out, g_xattn, w_q, w_o,
                g_moe, w_router, b_router):
    b, s, d = x.shape
    ts = SEQ_TILE
    ns = s // ts
    t = b * s
    m = kmem.shape[1]
    ne = w_router.shape[1]

    def full(shape):
        return pl.BlockSpec(shape, lambda i, j: (0,) * len(shape))

    tok_out = pl.BlockSpec((ts, LANES), lambda i, j: (i * ns + j, 0))
    return pl.pallas_call(
        _token_kernel,
        grid=(b, ns),
        in_specs=[
            pl.BlockSpec((None, ts, d), lambda i, j: (i, j, 0)),
            full((1, d)), full(w_in.shape), full(w_pool.shape), full((1, POOL_W)), full(conv_w.shape),
            full(w_out.shape), full((1, d)),
            pl.BlockSpec((None, m, d), lambda i, j: (i, 0, 0)),
            pl.BlockSpec((None, m, d), lambda i, j: (i, 0, 0)),
            full(w_q.shape), full(w_o.shape), full((1, d)), full(w_router.shape), full((1, ne)),
        ],
        out_specs=[
            pl.BlockSpec((ts * ROW_TILE, LANES), lambda i, j: (i * ns + j, 0)),
            tok_out, tok_out,
            pl.BlockSpec((1, ne), lambda i, j: (0, 0)),
        ],
        out_shape=[
            jax.ShapeDtypeStruct((t * ROW_TILE, LANES), F32),
            jax.ShapeDtypeStruct((t, LANES), jnp.int32),
            jax.ShapeDtypeStruct((t, LANES), F32),
            jax.ShapeDtypeStruct((1, ne), F32),
        ],
        scratch_shapes=[pltpu.VMEM((HALO, POOL_W), F32), pltpu.VMEM((8, CONV_W), F32), pltpu.VMEM((1, ne), F32)],
        compiler_params=pltpu.CompilerParams(dimension_semantics=("arbitrary", "arbitrary"),
                                             vmem_limit_bytes=VMEM_LIMIT),
        name="token_mix_attn_route",
    )(x, g_mix, w_in, w_pool, pool_scale, conv_w, w_out, g_xattn, kmem, vmem, w_q, w_o, g_moe, w_router, b_router)


def _expert_kernel(be_ref, nused_ref,
                   tok_hbm, dst_hbm, h2_hbm, gmoe_ref, wgu_ref, bgu_ref, wd_ref, bd_ref,
                   y4_hbm,
                   tok_s, dst_s, xbuf, ybuf, act, wgu_bf, wd_bf, gsem, ssem, tsem, dsem):
    j = pl.program_id(0)
    nb = pl.num_programs(0)
    rows = act.shape[0]
    d_ff = wd_ref.shape[0]
    n_used = nused_ref[0]
    block_rows = rows * ROW_TILE
    spare0 = y4_hbm.shape[0] - 3 * block_rows

    new_expert = jnp.logical_or(j == 0, be_ref[j] != be_ref[jnp.maximum(j - 1, 0)])

    @pl.when(jnp.logical_and(j < n_used, new_expert))
    def _():
        def cast(r, c):
            rs = pl.ds(pl.multiple_of(r * CAST_ROWS, CAST_ROWS), CAST_ROWS)
            wgu_bf[rs, :] = wgu_ref[rs, :].astype(BF16)
            wd_bf[rs, :] = wd_ref[rs, :].astype(BF16)
            return c
        lax.fori_loop(0, d_ff // CAST_ROWS, cast, 0)

    def tok_table(blk, slot):
        return pltpu.make_async_copy(tok_hbm.at[blk], tok_s.at[pl.ds(slot, 1)], tsem.at[slot])

    def dst_table(blk, slot):
        return pltpu.make_async_copy(dst_hbm.at[blk], dst_s.at[pl.ds(slot, 1)], dsem.at[slot])

    def gather_row(slot, i):
        src = pl.multiple_of(tok_s[slot, i], ROW_TILE)
        return pltpu.make_async_copy(h2_hbm.at[pl.ds(src, ROW_TILE)], xbuf.at[slot, pl.ds(i * ROW_TILE, ROW_TILE)],
                                     gsem.at[slot])

    def scatter_row(slot, i):
        dst = pl.multiple_of(dst_s[slot, i], ROW_TILE)
        return pltpu.make_async_copy(ybuf.at[slot, pl.ds(i * ROW_TILE, ROW_TILE)], y4_hbm.at[pl.ds(dst, ROW_TILE)],
                                     ssem.at[slot])

    def wait_gathers(slot):
        pltpu.make_async_copy(h2_hbm.at[pl.ds(0, block_rows)], xbuf.at[slot], gsem.at[slot]).wait()

    def wait_scatters(slot):
        pltpu.make_async_copy(ybuf.at[slot], y4_hbm.at[pl.ds(0, block_rows)], ssem.at[slot]).wait()

    def issue_all(row_copy, slot):
        def body(i, c):
            row_copy(slot, i).start()
            return c
        lax.fori_loop(0, rows, body, 0, unroll=8)

    @pl.when(j == 0)
    def _():
        ybuf[...] = jnp.zeros_like(ybuf)
        pltpu.make_async_copy(ybuf.at[0], y4_hbm.at[pl.ds(spare0, block_rows)], ssem.at[0]).start()
        fill = pltpu.make_async_copy(ybuf.at[1], y4_hbm.at[pl.ds(spare0 + block_rows, block_rows)], ssem.at[1])
        fill.start()
        fill.wait()
        first = tok_table(0, 0)
        first.start()
        first.wait()
        issue_all(gather_row, 0)
        tok_table(1, 1).start()
        dst_table(nb, 1).start()

    def step(q, p):
        tok_table(0, p).wait()
        dst_table(0, p).wait()
        tok_table(j + 2, q).start()
        dst_table(j, q).start()
        wait_gathers(q)
        chunks = [xbuf[q, pl.ds(c, rows, stride=ROW_TILE), :] for c in range(ROW_TILE)]
        ssq = chunks[0] * chunks[0]
        for xc in chunks[1:]:
            ssq = ssq + xc * xc
        scale = lax.rsqrt(jnp.sum(ssq, axis=-1, keepdims=True) / (ROW_TILE * LANES) + RMS_EPS)
        hn = jnp.concatenate(
            [(xc * scale * gmoe_ref[:, c * LANES:(c + 1) * LANES]).astype(BF16) for c, xc in enumerate(chunks)],
            axis=-1)
        n_chunks = d_ff // FF_CHUNK
        n_groups = 3 * n_chunks
        per_group = -(-2 * rows // n_groups)
        copies = [(gather_row, i) for i in range(rows)] + [(scatter_row, i) for i in range(rows)]
        groups = iter([copies[g * per_group:(g + 1) * per_group] for g in range(n_groups)])

        def start_group():
            for row_copy, i in next(groups):
                row_copy(p, i).start(priority=int(row_copy is scatter_row))

        for c in range(n_chunks):
            lo = c * FF_CHUNK
            glu = _dot(hn, wgu_bf[:, lo:lo + FF_CHUNK]) + bgu_ref[:, lo:lo + FF_CHUNK]
            start_group()
            lin = _dot(hn, wgu_bf[:, d_ff + lo:d_ff + lo + FF_CHUNK]) + bgu_ref[:, d_ff + lo:d_ff + lo + FF_CHUNK]
            start_group()
            glu = jnp.minimum(glu, SWIGLU_LIMIT)
            lin = jnp.clip(lin, -SWIGLU_LIMIT, SWIGLU_LIMIT)
            a = (lin + 1.0) * (glu * jax.nn.sigmoid(SWIGLU_ALPHA * glu))
            act[:, lo:lo + FF_CHUNK] = a.astype(BF16)
        wait_scatters(q)
        for c in range(n_chunks):
            lo = c * FF_CHUNK
            y = _dot(act[...], wd_bf[:, lo:lo + FF_CHUNK]) + bd_ref[:, lo:lo + FF_CHUNK]
            for k in range(FF_CHUNK // LANES):
                ybuf[q, pl.ds(lo // LANES + k, rows, stride=ROW_TILE), :] = y[:, k * LANES:(k + 1) * LANES]
            start_group()

        @pl.when(j == n_used - 1)
        def _():
            tok_table(0, q).wait()
            dst_table(0, q).wait()
            issue_all(scatter_row, q)
            wait_scatters(q)
            wait_scatters(p)
            wait_gathers(p)

    @pl.when(jnp.logical_and(j < n_used, j % 2 == 0))
    def _():
        step(0, 1)

    @pl.when(jnp.logical_and(j < n_used, j % 2 == 1))
    def _():
        step(1, 0)


def _expert_call(block_e, n_used, tok_slot, dst_slot, h2, g_moe, w_gu, b_gu, w_d, b_d, out_rows):
    d = g_moe.shape[1]
    nb = block_e.shape[0]
    rows = EXPERT_ROWS
    d_ff = w_d.shape[1]
    assert d == ROW_TILE * LANES
    any_spec = pl.BlockSpec(memory_space=pl.ANY)
    grid_spec = pltpu.PrefetchScalarGridSpec(
        num_scalar_prefetch=2,
        grid=(nb,),
        in_specs=[
            any_spec, any_spec, any_spec,
            pl.BlockSpec((1, d), lambda j, be, nu: (0, 0)),
            pl.BlockSpec((None, d, 2 * d_ff), lambda j, be, nu: (be[j], 0, 0)),
            pl.BlockSpec((None, 1, 2 * d_ff), lambda j, be, nu: (be[j], 0, 0)),
            pl.BlockSpec((None, d_ff, d), lambda j, be, nu: (be[j], 0, 0)),
            pl.BlockSpec((None, 1, d), lambda j, be, nu: (be[j], 0, 0)),
        ],
        out_specs=any_spec,
        scratch_shapes=[
            pltpu.SMEM((2, rows), jnp.int32), pltpu.SMEM((2, rows), jnp.int32),
            pltpu.VMEM((2, rows * ROW_TILE, LANES), F32), pltpu.VMEM((2, rows * ROW_TILE, LANES), F32),
            pltpu.VMEM((rows, d_ff), BF16), pltpu.VMEM((d, 2 * d_ff), BF16), pltpu.VMEM((d_ff, d), BF16),
            pltpu.SemaphoreType.DMA((2,)), pltpu.SemaphoreType.DMA((2,)), pltpu.SemaphoreType.DMA((2,)),
            pltpu.SemaphoreType.DMA((2,)),
        ],
    )
    return pl.pallas_call(
        _expert_kernel,
        grid_spec=grid_spec,
        out_shape=jax.ShapeDtypeStruct((out_rows * ROW_TILE, LANES), F32),
        compiler_params=pltpu.CompilerParams(dimension_semantics=("arbitrary",), vmem_limit_bytes=VMEM_LIMIT),
        name="expert_ffn",
    )(block_e, n_used, tok_slot, dst_slot, h2, g_moe, w_gu, b_gu, w_d, b_d)


def _combine_kernel(h2_ref, y0_ref, y1_ref, y2_ref, y3_ref, gate_ref, g_ref, o_ref):
    ts = o_ref.shape[0]
    gts = gate_ref[...]
    planes = (y0_ref, y1_ref, y2_ref, y3_ref)
    outs = []
    for c in range(ROW_TILE):
        chunk = pl.ds(c, ts, stride=ROW_TILE)
        acc = planes[0][chunk, :] * gts[:, 0:1]
        for k in range(1, TOP_K):
            acc = acc + planes[k][chunk, :] * gts[:, k:k + 1]
        outs.append(h2_ref[chunk, :] + acc)
    ssq = outs[0] * outs[0]
    for oc in outs[1:]:
        ssq = ssq + oc * oc
    scale = lax.rsqrt(jnp.sum(ssq, axis=-1, keepdims=True) / (ROW_TILE * LANES) + RMS_EPS)
    for c, oc in enumerate(outs):
        o_ref[:, c * LANES:(c + 1) * LANES] = oc * scale * g_ref[:, c * LANES:(c + 1) * LANES]


def _combine_call(h2, y4, gate, g_final):
    t = gate.shape[0]
    d = g_final.shape[1]
    ts = SEQ_TILE
    nt = t // ts
    blk = (ts * ROW_TILE, LANES)
    plane = lambda k: pl.BlockSpec(blk, lambda i: (k * nt + i, 0))
    return pl.pallas_call(
        _combine_kernel,
        grid=(nt,),
        in_specs=[pl.BlockSpec(blk, lambda i: (i, 0)), plane(0), plane(1), plane(2), plane(3),
                  pl.BlockSpec((ts, LANES), lambda i: (i, 0)), pl.BlockSpec((1, d), lambda i: (0, 0))],
        out_specs=pl.BlockSpec((ts, d), lambda i: (i, 0)),
        out_shape=jax.ShapeDtypeStruct((t, d), F32),
        compiler_params=pltpu.CompilerParams(dimension_semantics=("arbitrary",), vmem_limit_bytes=VMEM_LIMIT),
        name="combine_norm",
    )(h2, y4, y4, y4, y4, gate, g_final)


def _slot_tables(eidx, counts, t):
    rows = EXPERT_ROWS
    n_real = TOP_K * t
    assert n_real <= KEY_PAD_FLAG and rows <= KEY_PAD_FLAG
    n_slots = n_real + N_EXPERTS * rows
    nb = n_slots // rows
    padded = ((counts + rows - 1) // rows) * rows
    pad = padded - counts
    pstarts = jnp.cumsum(padded) - padded
    flat = jnp.arange(n_real, dtype=jnp.int32).reshape(t, TOP_K)
    real_keys = (eidx << KEY_EXPERT_SHIFT) | flat
    i = jnp.arange(rows, dtype=jnp.int32)[None, :]
    e = jnp.arange(N_EXPERTS, dtype=jnp.int32)[:, None]
    pad_keys = jnp.where(i < pad[:, None], (e << KEY_EXPERT_SHIFT) | KEY_PAD_FLAG | i, jnp.iinfo(jnp.int32).max)
    keys = lax.sort(jnp.concatenate([real_keys.reshape(-1), pad_keys.reshape(-1)]))
    real = (keys & KEY_PAD_FLAG) == 0
    a = keys & (KEY_PAD_FLAG - 1)
    slot = jnp.arange(n_slots, dtype=jnp.int32)
    tok_slot = jnp.where(real, a // TOP_K, 0) * ROW_TILE
    dst_slot = jnp.where(real, (a % TOP_K) * t + a // TOP_K, n_real + slot % (2 * rows))
    tok_slot = jnp.concatenate([tok_slot, jnp.zeros((rows,), jnp.int32)]).reshape(nb + 1, 1, rows)
    dst_slot = jnp.concatenate([dst_slot, n_real + 2 * rows + jnp.arange(rows, dtype=jnp.int32)]) * ROW_TILE
    dst_slot = dst_slot.reshape(nb + 1, 1, rows)
    n_used = (jnp.sum(padded) // rows).astype(jnp.int32)
    starts = jnp.arange(nb, dtype=jnp.int32) * rows
    block_e = jnp.sum((pstarts[None, :] <= starts[:, None]).astype(jnp.int32), axis=1) - 1
    block_e = jnp.where(jnp.arange(nb) < n_used, block_e, block_e[jnp.maximum(n_used - 1, 0)])
    return block_e, n_used.reshape(1), tok_slot, dst_slot, n_real + 3 * rows


def kernel(x, mem, g_mix, w_in, w_pool, pool_scale, conv_w, w_out, g_xattn, g_mem, w_q, w_k, w_v, w_o,
           g_moe, w_router, b_router, w_gate_up, b_gate_up, w_down, b_down, g_final):
    b, s, d = x.shape
    t = b * s
    assert g_mix.shape[0] == 1, "the combine kernel fuses the final norm, so exactly one layer is supported"
    assert s % SEQ_TILE == 0 and d == ROW_TILE * LANES
    row = lambda a: a[0].reshape(1, -1)
    kmem, vmem = _kv_call(mem, row(g_mem), w_k[0].astype(BF16), w_v[0].astype(BF16))
    h2, eidx, gate, counts = _token_call(
        x, kmem, vmem, row(g_mix), w_in[0].astype(BF16), w_pool[0].astype(BF16), row(pool_scale), conv_w[0],
        w_out[0].astype(BF16), row(g_xattn), w_q[0].astype(BF16), w_o[0].astype(BF16), row(g_moe),
        w_router[0].astype(BF16), row(b_router))
    block_e, n_used, tok_slot, dst_slot, out_rows = _slot_tables(eidx[:, :TOP_K], counts[0].astype(jnp.int32), t)
    y4 = _expert_call(block_e, n_used, tok_slot, dst_slot, h2, row(g_moe),
                      w_gate_up[0], b_gate_up[0][:, None, :], w_down[0], b_down[0][:, None, :], out_rows)
    return _combine_call(h2, y4, gate, g_final.reshape(1, -1)).reshape(b, s, d)
```

```python
import jax
import jax.numpy as jnp
from jax import lax
from jax.experimental import pallas as pl
from jax.experimental.pallas import tpu as pltpu

POOL_WINDOWS = (2, 4, 8, 16)
POOL_GROUP_W = 128
POOL_W = 512
CONV_W = 512
N_XHEADS = 4
N_EXPERTS = 32
TOP_K = 4
SWIGLU_LIMIT = 7.0
SWIGLU_ALPHA = 1.702
RMS_EPS = 1e-5

LANES = 128
ROW_TILE = 8
TOKEN_TILE = 1024
SEQ_TILE = 512
EXPERT_ROWS = 512
FF_CHUNK = 256
CAST_ROWS = 64
HALO = 16
VMEM_LIMIT = 56 * 1024 * 1024

KEY_ID_BITS = 18
KEY_PAD_FLAG = 1 << KEY_ID_BITS
KEY_EXPERT_SHIFT = KEY_ID_BITS + 1

F32 = jnp.float32
BF16 = jnp.bfloat16


def _rms(x, g):
    ms = jnp.mean(x * x, axis=-1, keepdims=True)
    return x * lax.rsqrt(ms + RMS_EPS) * g


def _dot(a, b):
    return jnp.dot(a, b, preferred_element_type=F32)


def _kv_kernel(mem_ref, g_ref, wk_ref, wv_ref, k_ref, v_ref):
    mn = _rms(mem_ref[...], g_ref[...]).astype(BF16)
    k_ref[...] = _dot(mn, wk_ref[...]).astype(BF16)
    v_ref[...] = _dot(mn, wv_ref[...]).astype(BF16)


def _kv_call(mem, g_mem, w_k, w_v):
    b, m, d = mem.shape
    full = lambda shape: pl.BlockSpec(shape, lambda i: (0,) * len(shape))
    return pl.pallas_call(
        _kv_kernel,
        grid=(b,),
        in_specs=[pl.BlockSpec((None, m, d), lambda i: (i, 0, 0)), full((1, d)), full((d, d)), full((d, d))],
        out_specs=[pl.BlockSpec((None, m, d), lambda i: (i, 0, 0))] * 2,
        out_shape=[jax.ShapeDtypeStruct((b, m, d), BF16)] * 2,
        compiler_params=pltpu.CompilerParams(dimension_semantics=("arbitrary",), vmem_limit_bytes=VMEM_LIMIT),
        name="kv_proj",
    )(mem, g_mem, w_k, w_v)


def _token_kernel(x_ref, gmix_ref, win_ref, wpool_ref, pscale_ref, convw_ref, wout_ref,
                  gx_ref, k_ref, v_ref, wq_ref, wo_ref, gmoe_ref, wr_ref, br_ref,
                  h2_ref, eidx_ref, gate_ref, cnt_ref,
                  uhalo, zhalo, carry):
    b = pl.program_id(0)
    s = pl.program_id(1)
    ts = x_ref.shape[0]

    @pl.when(s == 0)
    def _():
        uhalo[...] = jnp.zeros_like(uhalo)
        zhalo[...] = jnp.zeros_like(zhalo)

    @pl.when(jnp.logical_and(b == 0, s == 0))
    def _():
        carry[...] = jnp.zeros_like(carry)

    def sub_tile(r0, n, u_halo, z_halo):
        x = x_ref[r0:r0 + n, :]
        hn = _rms(x, gmix_ref[...]).astype(BF16)
        proj = _dot(hn, win_ref[...])
        u = proj[:, :POOL_W]
        gb = proj[:, POOL_W:POOL_W + CONV_W]
        gc = proj[:, POOL_W + CONV_W:POOL_W + 2 * CONV_W]
        v = proj[:, POOL_W + 2 * CONV_W:]

        pos = (s * ts + r0 + 1 + lax.broadcasted_iota(jnp.int32, (n, 1), 0)).astype(F32)
        ucat = jnp.concatenate([u_halo, u], axis=0)
        mixed = []
        for g, w in enumerate(POOL_WINDOWS):
            cols = slice(g * POOL_GROUP_W, (g + 1) * POOL_GROUP_W)
            win = ucat[:, cols]
            span = 1
            while span < w:
                win = win + pltpu.roll(win, span, 0)
                span *= 2
            inv_cnt = 1.0 / jnp.minimum(pos, float(w))
            dg = win[HALO:, :] * inv_cnt - u[:, cols]
            yg = _dot(dg.astype(BF16), wpool_ref[g]) * pscale_ref[:, cols]
            mixed.append(yg.astype(BF16))

        z = gc * v
        zcat = jnp.concatenate([z_halo, z], axis=0)
        z1 = pltpu.roll(zcat, 1, 0)[8:, :]
        z2 = pltpu.roll(zcat, 2, 0)[8:, :]
        zc = convw_ref[0:1, :] * z2 + convw_ref[1:2, :] * z1 + convw_ref[2:3, :] * z
        mixed.append((gb * zc).astype(BF16))
        h1 = x + _dot(jnp.concatenate(mixed, axis=-1), wout_ref[...])

        hq = _rms(h1, gx_ref[...]).astype(BF16)
        q = _dot(hq, wq_ref[...]).astype(BF16)
        hd = q.shape[1] // N_XHEADS
        attn = None
        for h in range(N_XHEADS):
            cols = slice(h * hd, (h + 1) * hd)
            sc = lax.dot_general(q[:, cols], k_ref[:, cols], (((1,), (1,)), ((), ())),
                                 preferred_element_type=F32) * (hd ** -0.5)
            sc = sc - jnp.max(sc, axis=-1, keepdims=True)
            p = jnp.exp(sc)
            p = p / jnp.sum(p, axis=-1, keepdims=True)
            oh = _dot(p.astype(BF16), v_ref[:, cols])
            part = _dot(oh.astype(BF16), wo_ref[cols, :])
            attn = part if attn is None else attn + part
        h2 = h1 + attn
        for c in range(ROW_TILE):
            h2_ref[pl.ds(r0 * ROW_TILE + c, n, stride=ROW_TILE), :] = h2[:, c * LANES:(c + 1) * LANES]

        hm = _rms(h2, gmoe_ref[...]).astype(BF16)
        logits = _dot(hm, wr_ref[...]) + br_ref[...]
        ne = logits.shape[1]
        lane = lax.broadcasted_iota(jnp.int32, (n, ne), 1).astype(F32)
        sel = jnp.zeros((n, ne), F32)
        vals, idxs = [], []
        for _ in range(TOP_K):
            m = jnp.max(logits, axis=-1, keepdims=True)
            idx = jnp.min(jnp.where(logits == m, lane, float(ne)), axis=-1, keepdims=True)
            hot = lane == idx
            logits = jnp.where(hot, -jnp.inf, logits)
            sel = jnp.where(hot, 1.0, sel)
            vals.append(m)
            idxs.append(idx)
        exps = [jnp.exp(vk - vals[0]) for vk in vals]
        denom = exps[0]
        for ek in exps[1:]:
            denom = denom + ek
        gates = [ek / denom for ek in exps]

        out_lane = lax.broadcasted_iota(jnp.int32, (n, LANES), 1)

        def spread(cols_):
            acc = jnp.zeros((n, LANES), F32)
            for kk, c in enumerate(cols_):
                acc = jnp.where(out_lane == kk, c, acc)
            return acc

        eidx_ref[r0:r0 + n, :] = spread(idxs).astype(jnp.int32)
        gate_ref[r0:r0 + n, :] = spread(gates)
        return u[n - HALO:, :], z[n - 8:, :], jnp.sum(sel, axis=0, keepdims=True)

    u_halo, z_halo, counts = sub_tile(0, ts, uhalo[...], zhalo[...])
    uhalo[...] = u_halo
    zhalo[...] = z_halo
    carry[...] = carry[...] + counts
    cnt_ref[...] = carry[...]


def _token_call(x, kmem, vmem, g_mix, w_in, w_pool, pool_scale, conv_w, w_out, g_xattn, w_q, w_o,
                g_moe, w_router, b_router):
    b, s, d = x.shape
    ts = TOKEN_TILE
    ns = s // ts
    t = b * s
    m = kmem.shape[1]
    ne = w_router.shape[1]

    def full(shape):
        return pl.BlockSpec(shape, lambda i, j: (0,) * len(shape))

    tok_out = pl.BlockSpec((ts, LANES), lambda i, j: (i * ns + j, 0))
    return pl.pallas_call(
        _token_kernel,
        grid=(b, ns),
        in_specs=[
            pl.BlockSpec((None, ts, d), lambda i, j: (i, j, 0)),
            full((1, d)), full(w_in.shape), full(w_pool.shape), full((1, POOL_W)), full(conv_w.shape),
            full(w_out.shape), full((1, d)),
            pl.BlockSpec((None, m, d), lambda i, j: (i, 0, 0)),
            pl.BlockSpec((None, m, d), lambda i, j: (i, 0, 0)),
            full(w_q.shape), full(w_o.shape), full((1, d)), full(w_router.shape), full((1, ne)),
        ],
        out_specs=[
            pl.BlockSpec((ts * ROW_TILE, LANES), lambda i, j: (i * ns + j, 0)),
            tok_out, tok_out,
            pl.BlockSpec((1, ne), lambda i, j: (0, 0)),
        ],
        out_shape=[
            jax.ShapeDtypeStruct((t * ROW_TILE, LANES), F32),
            jax.ShapeDtypeStruct((t, LANES), jnp.int32),
            jax.ShapeDtypeStruct((t, LANES), F32),
            jax.ShapeDtypeStruct((1, ne), F32),
        ],
        scratch_shapes=[pltpu.VMEM((HALO, POOL_W), F32), pltpu.VMEM((8, CONV_W), F32), pltpu.VMEM((1, ne), F32)],
        compiler_params=pltpu.CompilerParams(dimension_semantics=("arbitrary", "arbitrary"),
                                             vmem_limit_bytes=VMEM_LIMIT),
        name="token_mix_attn_route",
    )(x, g_mix, w_in, w_pool, pool_scale, conv_w, w_out, g_xattn, kmem, vmem, w_q, w_o, g_moe, w_router, b_router)


def _expert_kernel(be_ref, nused_ref,
                   tok_hbm, dst_hbm, h2_hbm, gmoe_ref, wgu_ref, bgu_ref, wd_ref, bd_ref,
                   y4_hbm,
                   tok_s, dst_s, xbuf, ybuf, act, wgu_bf, wd_bf, gsem, ssem, tsem, dsem):
    j = pl.program_id(0)
    nb = pl.num_programs(0)
    rows = act.shape[0]
    d_ff = wd_ref.shape[0]
    n_used = nused_ref[0]
    block_rows = rows * ROW_TILE
    spare0 = y4_hbm.shape[0] - 3 * block_rows

    new_expert = jnp.logical_or(j == 0, be_ref[j] != be_ref[jnp.maximum(j - 1, 0)])

    @pl.when(jnp.logical_and(j < n_used, new_expert))
    def _():
        def cast(r, c):
            rs = pl.ds(pl.multiple_of(r * CAST_ROWS, CAST_ROWS), CAST_ROWS)
            wgu_bf[rs, :] = wgu_ref[rs, :].astype(BF16)
            wd_bf[rs, :] = wd_ref[rs, :].astype(BF16)
            return c
        lax.fori_loop(0, d_ff // CAST_ROWS, cast, 0)

    def tok_table(blk, slot):
        return pltpu.make_async_copy(tok_hbm.at[blk], tok_s.at[pl.ds(slot, 1)], tsem.at[slot])

    def dst_table(blk, slot):
        return pltpu.make_async_copy(dst_hbm.at[blk], dst_s.at[pl.ds(slot, 1)], dsem.at[slot])

    def gather_row(slot, i):
        src = pl.multiple_of(tok_s[slot, i], ROW_TILE)
        return pltpu.make_async_copy(h2_hbm.at[pl.ds(src, ROW_TILE)], xbuf.at[slot, pl.ds(i * ROW_TILE, ROW_TILE)],
                                     gsem.at[slot])

    def scatter_row(slot, i):
        dst = pl.multiple_of(dst_s[slot, i], ROW_TILE)
        return pltpu.make_async_copy(ybuf.at[slot, pl.ds(i * ROW_TILE, ROW_TILE)], y4_hbm.at[pl.ds(dst, ROW_TILE)],
                                     ssem.at[slot])

    def wait_gathers(slot):
        pltpu.make_async_copy(h2_hbm.at[pl.ds(0, block_rows)], xbuf.at[slot], gsem.at[slot]).wait()

    def wait_scatters(slot):
        pltpu.make_async_copy(ybuf.at[slot], y4_hbm.at[pl.ds(0, block_rows)], ssem.at[slot]).wait()

    def issue_all(row_copy, slot):
        def body(i, c):
            row_copy(slot, i).start()
            return c
        lax.fori_loop(0, rows, body, 0, unroll=8)

    @pl.when(j == 0)
    def _():
        ybuf[...] = jnp.zeros_like(ybuf)
        pltpu.make_async_copy(ybuf.at[0], y4_hbm.at[pl.ds(spare0, block_rows)], ssem.at[0]).start()
        fill = pltpu.make_async_copy(ybuf.at[1], y4_hbm.at[pl.ds(spare0 + block_rows, block_rows)], ssem.at[1])
        fill.start()
        fill.wait()
        first = tok_table(0, 0)
        first.start()
        first.wait()
        issue_all(gather_row, 0)
        tok_table(1, 1).start()
        dst_table(nb, 1).start()

    def step(q, p):
        tok_table(0, p).wait()
        dst_table(0, p).wait()
        tok_table(j + 2, q).start()
        dst_table(j, q).start()
        wait_gathers(q)
        chunks = [xbuf[q, pl.ds(c, rows, stride=ROW_TILE), :] for c in range(ROW_TILE)]
        ssq = chunks[0] * chunks[0]
        for xc in chunks[1:]:
            ssq = ssq + xc * xc
        scale = lax.rsqrt(jnp.sum(ssq, axis=-1, keepdims=True) / (ROW_TILE * LANES) + RMS_EPS)
        hn = jnp.concatenate(
            [(xc * scale * gmoe_ref[:, c * LANES:(c + 1) * LANES]).astype(BF16) for c, xc in enumerate(chunks)],
            axis=-1)
        n_chunks = d_ff // FF_CHUNK
        n_groups = 2 * n_chunks
        per_group = -(-2 * rows // n_groups)
        copies = [(gather_row, i) for i in range(rows)] + [(scatter_row, i) for i in range(rows)]
        groups = iter([copies[g * per_group:(g + 1) * per_group] for g in range(n_groups)])

        def start_group():
            for row_copy, i in next(groups):
                row_copy(p, i).start(priority=1)

        for c in range(n_chunks):
            lo = c * FF_CHUNK
            glu = _dot(hn, wgu_bf[:, lo:lo + FF_CHUNK]) + bgu_ref[:, lo:lo + FF_CHUNK]
            start_group()
            lin = _dot(hn, wgu_bf[:, d_ff + lo:d_ff + lo + FF_CHUNK]) + bgu_ref[:, d_ff + lo:d_ff + lo + FF_CHUNK]
            start_group()
            glu = jnp.minimum(glu, SWIGLU_LIMIT)
            lin = jnp.clip(lin, -SWIGLU_LIMIT, SWIGLU_LIMIT)
            a = (lin + 1.0) * (glu * jax.nn.sigmoid(SWIGLU_ALPHA * glu))
            act[:, lo:lo + FF_CHUNK] = a.astype(BF16)
        wait_scatters(q)
        y = _dot(act[...], wd_bf[...]) + bd_ref[...]
        for c in range(ROW_TILE):
            ybuf[q, pl.ds(c, rows, stride=ROW_TILE), :] = y[:, c * LANES:(c + 1) * LANES]

        @pl.when(j == n_used - 1)
        def _():
            tok_table(0, q).wait()
            dst_table(0, q).wait()
            issue_all(scatter_row, q)
            wait_scatters(q)
            wait_scatters(p)
            wait_gathers(p)

    @pl.when(jnp.logical_and(j < n_used, j % 2 == 1))
    def _():
        step(1, 0)

    @pl.when(jnp.logical_and(j < n_used, j % 2 == 0))
    def _():
        step(0, 1)


def _expert_call(block_e, n_used, tok_slot, dst_slot, h2, g_moe, w_gu, b_gu, w_d, b_d, out_rows):
    d = g_moe.shape[1]
    nb = block_e.shape[0]
    rows = EXPERT_ROWS
    d_ff = w_d.shape[1]
    assert d == ROW_TILE * LANES
    any_spec = pl.BlockSpec(memory_space=pl.ANY)
    grid_spec = pltpu.PrefetchScalarGridSpec(
        num_scalar_prefetch=2,
        grid=(nb,),
        in_specs=[
            any_spec, any_spec, any_spec,
            pl.BlockSpec((1, d), lambda j, be, nu: (0, 0)),
            pl.BlockSpec((None, d, 2 * d_ff), lambda j, be, nu: (be[j], 0, 0)),
            pl.BlockSpec((None, 1, 2 * d_ff), lambda j, be, nu: (be[j], 0, 0)),
            pl.BlockSpec((None, d_ff, d), lambda j, be, nu: (be[j], 0, 0)),
            pl.BlockSpec((None, 1, d), lambda j, be, nu: (be[j], 0, 0)),
        ],
        out_specs=any_spec,
        scratch_shapes=[
            pltpu.SMEM((2, rows), jnp.int32), pltpu.SMEM((2, rows), jnp.int32),
            pltpu.VMEM((2, rows * ROW_TILE, LANES), F32), pltpu.VMEM((2, rows * ROW_TILE, LANES), F32),
            pltpu.VMEM((rows, d_ff), BF16), pltpu.VMEM((d, 2 * d_ff), BF16), pltpu.VMEM((d_ff, d), BF16),
            pltpu.SemaphoreType.DMA((2,)), pltpu.SemaphoreType.DMA((2,)), pltpu.SemaphoreType.DMA((2,)),
            pltpu.SemaphoreType.DMA((2,)),
        ],
    )
    return pl.pallas_call(
        _expert_kernel,
        grid_spec=grid_spec,
        out_shape=jax.ShapeDtypeStruct((out_rows * ROW_TILE, LANES), F32),
        compiler_params=pltpu.CompilerParams(dimension_semantics=("arbitrary",), vmem_limit_bytes=VMEM_LIMIT),
        name="expert_ffn",
    )(block_e, n_used, tok_slot, dst_slot, h2, g_moe, w_gu, b_gu, w_d, b_d)


def _combine_kernel(h2_ref, y0_ref, y1_ref, y2_ref, y3_ref, gate_ref, g_ref, o_ref):
    ts = o_ref.shape[0]
    gts = gate_ref[...]
    planes = (y0_ref, y1_ref, y2_ref, y3_ref)
    outs = []
    for c in range(ROW_TILE):
        chunk = pl.ds(c, ts, stride=ROW_TILE)
        acc = planes[0][chunk, :] * gts[:, 0:1]
        for k in range(1, TOP_K):
            acc = acc + planes[k][chunk, :] * gts[:, k:k + 1]
        outs.append(h2_ref[chunk, :] + acc)
    ssq = outs[0] * outs[0]
    for oc in outs[1:]:
        ssq = ssq + oc * oc
    scale = lax.rsqrt(jnp.sum(ssq, axis=-1, keepdims=True) / (ROW_TILE * LANES) + RMS_EPS)
    for c, oc in enumerate(outs):
        o_ref[:, c * LANES:(c + 1) * LANES] = oc * scale * g_ref[:, c * LANES:(c + 1) * LANES]


def _combine_call(h2, y4, gate, g_final):
    t = gate.shape[0]
    d = g_final.shape[1]
    ts = SEQ_TILE
    nt = t // ts
    blk = (ts * ROW_TILE, LANES)
    plane = lambda k: pl.BlockSpec(blk, lambda i: (k * nt + i, 0))
    return pl.pallas_call(
        _combine_kernel,
        grid=(nt,),
        in_specs=[pl.BlockSpec(blk, lambda i: (i, 0)), plane(0), plane(1), plane(2), plane(3),
                  pl.BlockSpec((ts, LANES), lambda i: (i, 0)), pl.BlockSpec((1, d), lambda i: (0, 0))],
        out_specs=pl.BlockSpec((ts, d), lambda i: (i, 0)),
        out_shape=jax.ShapeDtypeStruct((t, d), F32),
        compiler_params=pltpu.CompilerParams(dimension_semantics=("arbitrary",), vmem_limit_bytes=VMEM_LIMIT),
        name="combine_norm",
    )(h2, y4, y4, y4, y4, gate, g_final)


def _slot_tables(eidx, counts, t):
    rows = EXPERT_ROWS
    n_real = TOP_K * t
    assert n_real <= KEY_PAD_FLAG and rows <= KEY_PAD_FLAG
    n_slots = n_real + N_EXPERTS * rows
    nb = n_slots // rows
    padded = ((counts + rows - 1) // rows) * rows
    pad = padded - counts
    pstarts = jnp.cumsum(padded) - padded
    flat = jnp.arange(n_real, dtype=jnp.int32).reshape(t, TOP_K)
    real_keys = (eidx << KEY_EXPERT_SHIFT) | flat
    i = jnp.arange(rows, dtype=jnp.int32)[None, :]
    e = jnp.arange(N_EXPERTS, dtype=jnp.int32)[:, None]
    pad_keys = jnp.where(i < pad[:, None], (e << KEY_EXPERT_SHIFT) | KEY_PAD_FLAG | i, jnp.iinfo(jnp.int32).max)
    keys = lax.sort(jnp.concatenate([real_keys.reshape(-1), pad_keys.reshape(-1)]))
    real = (keys & KEY_PAD_FLAG) == 0
    a = keys & (KEY_PAD_FLAG - 1)
    slot = jnp.arange(n_slots, dtype=jnp.int32)
    tok_slot = jnp.where(real, a // TOP_K, 0) * ROW_TILE
    dst_slot = jnp.where(real, (a % TOP_K) * t + a // TOP_K, n_real + slot % (2 * rows))
    tok_slot = jnp.concatenate([tok_slot, jnp.zeros((rows,), jnp.int32)]).reshape(nb + 1, 1, rows)
    dst_slot = jnp.concatenate([dst_slot, n_real + 2 * rows + jnp.arange(rows, dtype=jnp.int32)]) * ROW_TILE
    dst_slot = dst_slot.reshape(nb + 1, 1, rows)
    n_used = (jnp.sum(padded) // rows).astype(jnp.int32)
    starts = jnp.arange(nb, dtype=jnp.int32) * rows
    block_e = jnp.sum((pstarts[None, :] <= starts[:, None]).astype(jnp.int32), axis=1) - 1
    block_e = jnp.where(jnp.arange(nb) < n_used, block_e, block_e[jnp.maximum(n_used - 1, 0)])
    return block_e, n_used.reshape(1), tok_slot, dst_slot, n_real + 3 * rows


def kernel(x, mem, g_mix, w_in, w_pool, pool_scale, conv_w, w_out, g_xattn, g_mem, w_q, w_k, w_v, w_o,
           g_moe, w_router, b_router, w_gate_up, b_gate_up, w_down, b_down, g_final):
    b, s, d = x.shape
    t = b * s
    assert g_mix.shape[0] == 1, "the combine kernel fuses the final norm, so exactly one layer is supported"
    assert s % TOKEN_TILE == 0 and t % SEQ_TILE == 0 and d == ROW_TILE * LANES
    row = lambda a: a[0].reshape(1, -1)
    kmem, vmem = _kv_call(mem, row(g_mem), w_k[0].astype(BF16), w_v[0].astype(BF16))
    h2, eidx, gate, counts = _token_call(
        x, kmem, vmem, row(g_mix), w_in[0].astype(BF16), w_pool[0].astype(BF16), row(pool_scale), conv_w[0],
        w_out[0].astype(BF16), row(g_xattn), w_q[0].astype(BF16), w_o[0].astype(BF16), row(g_moe),
        w_router[0].astype(BF16), row(b_router))
    block_e, n_used, tok_slot, dst_slot, out_rows = _slot_tables(eidx[:, :TOP_K], counts[0].astype(jnp.int32), t)
    y4 = _expert_call(block_e, n_used, tok_slot, dst_slot, h2, row(g_moe),
                      w_gate_up[0], b_gate_up[0][:, None, :], w_down[0], b_down[0][:, None, :], out_rows)
    return _combine_call(h2, y4, gate, g_final.reshape(1, -1)).reshape(b, s, d)
```
